```python
import jax, jax.numpy as jnp
from jax import lax
import numpy as np

D_MODEL = 2048
BATCH = 2
SEQ = 16384
DEPTH = 2

POOL_WIDTH = D_MODEL // 2
POOL_WINDOWS = (2, 4, 8, 16)
POOL_GROUP = POOL_WIDTH // len(POOL_WINDOWS)
HEAD_DIM = 64
N_HEADS = D_MODEL // 128
GQA = 8
N_KV = N_HEADS // GQA
WINDOW = 128
BLK = 128
CONV_WIDTH = D_MODEL // 2
CONV_K = 3
D_FF = ((8 * D_MODEL // 3 + 255) // 256) * 256
FFN_CONV_K = 3
N_BRANCH = 3
RMS_EPS = 1e-6
NEG_INF = -1e30

SPLITS = (POOL_WIDTH, N_HEADS * HEAD_DIM, N_KV * HEAD_DIM, N_KV * HEAD_DIM,
          CONV_WIDTH, CONV_WIDTH, CONV_WIDTH, N_BRANCH * D_MODEL)
N_IN = sum(SPLITS)

kernel_name = "hybrid_pool_swa_shortconv_gated"


def rms_norm(x, g):
    x32 = x.astype(jnp.float32)
    y = x32 * lax.rsqrt(jnp.mean(x32 * x32, axis=-1, keepdims=True) + RMS_EPS)
    return (y * g.astype(jnp.float32)).astype(x.dtype)


def causal_dwconv(z, w, b=None):
    K = w.shape[0]
    S = z.shape[1]
    zp = jnp.pad(z, ((0, 0), (K - 1, 0), (0, 0)))
    y = zp[:, 0:S] * w[0]
    for k in range(1, K):
        y = y + zp[:, k:k + S] * w[k]
    if b is not None:
        y = y + b
    return y


def pool_mixer(u, w_grp, scale):
    Bsz, S, C = u.shape
    u32 = u.astype(jnp.float32)
    cs = jnp.concatenate([jnp.zeros((Bsz, 1, C), jnp.float32), jnp.cumsum(u32, axis=1)], axis=1)
    t = jnp.arange(S)
    outs = []
    for gi, w in enumerate(POOL_WINDOWS):
        sl = slice(gi * POOL_GROUP, (gi + 1) * POOL_GROUP)
        csg = cs[..., sl]
        lo = jnp.maximum(t + 1 - w, 0)
        win_sum = csg[:, t + 1] - csg[:, lo]
        cnt = jnp.minimum(t + 1, w).astype(jnp.float32)[None, :, None]
        outs.append(win_sum / cnt - u32[..., sl])
    p = jnp.stack(outs, axis=2).astype(u.dtype)
    y = jnp.einsum('bsgc,gcd->bsgd', p, w_grp).reshape(Bsz, S, C)
    return y * scale


def alibi_slopes():
    h = jnp.arange(1, N_HEADS + 1, dtype=jnp.float32)
    return jnp.exp2(-8.0 * h / N_HEADS).reshape(N_KV, GQA)


def sliding_window_attention(q, k, v, sinks):
    Bsz, S, _ = q.shape
    NB = S // BLK
    q = q.reshape(Bsz, NB, BLK, N_KV, GQA, HEAD_DIM)
    k = k.reshape(Bsz, S, N_KV, HEAD_DIM)
    v = v.reshape(Bsz, S, N_KV, HEAD_DIM)

    def band(z):
        prev = jnp.pad(z, ((0, 0), (BLK, 0), (0, 0), (0, 0)))[:, :S]
        return jnp.concatenate([prev.reshape(Bsz, NB, BLK, N_KV, HEAD_DIM),
                                z.reshape(Bsz, NB, BLK, N_KV, HEAD_DIM)], axis=2)

    kb, vb = band(k), band(v)
    s = jnp.einsum('bnqkgd,bnjkd->bnkgqj', q, kb).astype(jnp.float32) * (HEAD_DIM ** -0.5)
    i = jnp.arange(BLK)[:, None]
    j = jnp.arange(2 * BLK)[None, :]
    dist = i + BLK - j
    key_pos = jnp.arange(NB)[:, None, None] * BLK - BLK + j[None]
    valid = (dist >= 0) & (dist < WINDOW) & (key_pos >= 0)
    s = s - alibi_slopes()[:, :, None, None] * dist.astype(jnp.float32)
    s = jnp.where(valid[None, :, None, None], s, NEG_INF)
    sink = sinks.astype(jnp.float32).reshape(N_KV, GQA)[:, :, None, None]
    m = jnp.maximum(jnp.max(s, axis=-1, keepdims=True), sink)
    p = jnp.exp(s - m)
    denom = jnp.sum(p, axis=-1, keepdims=True) + jnp.exp(sink - m)
    p = (p / denom).astype(v.dtype)
    o = jnp.einsum('bnkgqj,bnjkd->bnqkgd', p, vb)
    return o.reshape(Bsz, S, N_HEADS * HEAD_DIM)


def setup_inputs(seed: int = 0) -> dict:
    key = jax.random.key(seed)
    ks = jax.random.split(key, 20)
    f32 = jnp.float32
    L, D = DEPTH, D_MODEL
    nrm = lambda k, shape, s: jax.random.normal(k, shape, f32) * s
    return {
        "x": jax.random.normal(ks[0], (BATCH, SEQ, D), f32),
        "g_mix": 1.0 + nrm(ks[1], (L, D), 0.05),
        "w_in": nrm(ks[2], (L, D, N_IN), D ** -0.5),
        "b_gate": nrm(ks[3], (L, N_BRANCH * D), 0.02),
        "w_pool_grp": nrm(ks[4], (L, len(POOL_WINDOWS), POOL_GROUP, POOL_GROUP), POOL_GROUP ** -0.5),
        "pool_scale": 1.0 + nrm(ks[5], (L, POOL_WIDTH), 0.1),
        "w_pool_out": nrm(ks[6], (L, POOL_WIDTH, D), POOL_WIDTH ** -0.5),
        "sinks": nrm(ks[7], (L, N_HEADS), 0.5),
        "w_attn_out": nrm(ks[8], (L, N_HEADS * HEAD_DIM, D), (N_HEADS * HEAD_DIM) ** -0.5),
        "w_conv_mix": nrm(ks[9], (L, CONV_K, CONV_WIDTH), CONV_K ** -0.5),
        "w_conv_out": nrm(ks[10], (L, CONV_WIDTH, D), CONV_WIDTH ** -0.5),
        "w_o": nrm(ks[11], (L, D, D), D ** -0.5),
        "g_ffn": 1.0 + nrm(ks[12], (L, D), 0.05),
        "w_up": nrm(ks[13], (L, D, 2 * D_FF), D ** -0.5),
        "w_ffn_conv": nrm(ks[14], (L, FFN_CONV_K, 2 * D_FF), FFN_CONV_K ** -0.5),
        "b_ffn_conv": nrm(ks[15], (L, 2 * D_FF), 0.02),
        "w_down": nrm(ks[16], (L, D_FF, D), D_FF ** -0.5),
        "g_final": 1.0 + nrm(ks[17], (D,), 0.05),
    }


def reference(x, g_mix, w_in, b_gate, w_pool_grp, pool_scale, w_pool_out, sinks, w_attn_out,
              w_conv_mix, w_conv_out, w_o, g_ffn, w_up, w_ffn_conv, b_ffn_conv, w_down, g_final):
    split_idx = np.cumsum(SPLITS)[:-1].tolist()
    for l in range(DEPTH):
        h = rms_norm(x, g_mix[l])
        proj = h @ w_in[l]
        u_pool, q, k, v, c_b, c_c, c_h, gate_logits = jnp.split(proj, split_idx, axis=-1)
        gates = jax.nn.sigmoid((gate_logits + b_gate[l]).astype(jnp.float32)).astype(x.dtype)
        g_a, g_b, g_c = jnp.split(gates, N_BRANCH, axis=-1)
        y_a = pool_mixer(u_pool, w_pool_grp[l], pool_scale[l]) @ w_pool_out[l]
        y_b = sliding_window_attention(q, k, v, sinks[l]) @ w_attn_out[l]
        y_c = (c_b * causal_dwconv(c_c * c_h, w_conv_mix[l])) @ w_conv_out[l]
        merged = g_a * y_a + g_b * y_b + g_c * y_c
        x = x + merged @ w_o[l]
        h2 = rms_norm(x, g_ffn[l])
        u = causal_dwconv(h2 @ w_up[l], w_ffn_conv[l], b_ffn_conv[l])
        u_gate, u_val = jnp.split(u, 2, axis=-1)
        x = x + (jax.nn.silu(u_gate) * u_val) @ w_down[l]
    return rms_norm(x, g_final)
```

```python
import functools

import jax
import jax.numpy as jnp
from jax import lax
from jax.experimental import pallas as pl
from jax.experimental.pallas import tpu as pltpu

D_MODEL = 2048
BATCH = 2
SEQ = 16384
DEPTH = 2
TOKENS = BATCH * SEQ

POOL_WIDTH = D_MODEL // 2
POOL_WINDOWS = (2, 4, 8, 16)
POOL_GROUP = POOL_WIDTH // len(POOL_WINDOWS)
HEAD_DIM = 64
N_HEADS = D_MODEL // 128
GQA = 8
N_KV = N_HEADS // GQA
WINDOW = 128
BLK = 128
CONV_WIDTH = D_MODEL // 2
D_FF = ((8 * D_MODEL // 3 + 255) // 256) * 256
N_BRANCH = 3
RMS_EPS = 1e-6
NEG_INF = -1e30

ATTN_WIDTH = N_HEADS * HEAD_DIM
KV_WIDTH = N_KV * HEAD_DIM
COL_U = 0
COL_Q = COL_U + POOL_WIDTH
COL_K = COL_Q + ATTN_WIDTH
COL_V = COL_K + KV_WIDTH
COL_CB = COL_V + KV_WIDTH
COL_CC = COL_CB + CONV_WIDTH
COL_CH = COL_CC + CONV_WIDTH
COL_GATE = COL_CH + CONV_WIDTH
N_MIX_IN = COL_GATE
N_GATE = N_BRANCH * D_MODEL

LANES = 128
SUBLANES_F32 = 8
SUBLANES_BF16 = 16
VMEM_LIMIT_BYTES = 56 * 1024 * 1024

BF16 = jnp.bfloat16
F32 = jnp.float32

IN_PROJ_TN = 768
IN_PROJ_TM = 1024
ATTN_TQ = 512
MIX_TM = 512
OPROJ_TM = 512
UP_TM = 1024
UP_TN = 512
DOWN_TM = 256
NORM_TM = 1024


def _params(*sem):
    return pltpu.CompilerParams(dimension_semantics=sem, vmem_limit_bytes=VMEM_LIMIT_BYTES)


def _resident(shape, index_map):
    return pl.BlockSpec(shape, index_map, pipeline_mode=pl.Buffered(1))


def _rms_scale(x, g):
    ms = jnp.mean(x * x, axis=-1, keepdims=True)
    return x * lax.rsqrt(ms + RMS_EPS) * g


def _rmsnorm_kernel(x_ref, g_ref, o_ref):
    o_ref[...] = _rms_scale(x_ref[...], g_ref[...]).astype(o_ref.dtype)


def _rmsnorm_bf16(x, g3, layer):
    tm = NORM_TM
    return pl.pallas_call(
        _rmsnorm_kernel,
        grid=(TOKENS // tm,),
        in_specs=[pl.BlockSpec((tm, D_MODEL), lambda i: (i, 0)),
                  pl.BlockSpec((None, 1, D_MODEL), lambda i: (layer, 0, 0))],
        out_specs=pl.BlockSpec((tm, D_MODEL), lambda i: (i, 0)),
        out_shape=jax.ShapeDtypeStruct((TOKENS, D_MODEL), BF16),
        compiler_params=_params("arbitrary"),
        name="rmsnorm_in",
    )(x, g3)


def _in_proj_kernel(h_ref, w_ref, o_ref, wbf_ref):
    @pl.when(pl.program_id(1) == 0)
    def _():
        wbf_ref[...] = w_ref[...].astype(BF16)

    acc = jnp.dot(h_ref[...], wbf_ref[...], preferred_element_type=F32)
    o_ref[...] = acc.astype(o_ref.dtype)


def _in_proj_gate_kernel(h_ref, w_ref, b_ref, o_ref, wbf_ref):
    @pl.when(pl.program_id(1) == 0)
    def _():
        wbf_ref[...] = w_ref[...].astype(BF16)

    z = jnp.dot(h_ref[...], wbf_ref[...], preferred_element_type=F32) + b_ref[...]
    o_ref[...] = (1.0 / (1.0 + jnp.exp(-z))).astype(o_ref.dtype)


def _in_proj(h, w_in, b_gate3, layer, gates):
    tm, tn = IN_PROJ_TM, IN_PROJ_TN
    n_out = N_GATE if gates else N_MIX_IN
    col0 = (COL_GATE if gates else 0) // tn
    in_specs = [pl.BlockSpec((tm, D_MODEL), lambda j, m: (m, 0)),
                pl.BlockSpec((None, D_MODEL, tn), lambda j, m: (layer, 0, j + col0))]
    args = [h, w_in]
    if gates:
        in_specs.append(pl.BlockSpec((None, 1, tn), lambda j, m: (layer, 0, j)))
        args.append(b_gate3)
    return pl.pallas_call(
        _in_proj_gate_kernel if gates else _in_proj_kernel,
        grid=(n_out // tn, TOKENS // tm),
        in_specs=in_specs,
        out_specs=pl.BlockSpec((tm, tn), lambda j, m: (m, j)),
        out_shape=jax.ShapeDtypeStruct((TOKENS, n_out), BF16),
        scratch_shapes=[pltpu.VMEM((D_MODEL, tn), BF16)],
        compiler_params=_params("arbitrary", "arbitrary"),
        name="in_proj_gates" if gates else "in_proj_mix",
    )(*args)


def _attn_kernel(sink_ref, q_ref, kv_ref, halo_ref, o_ref, kvext_ref, *, layer, tq):
    n_qb = tq // BLK
    i = pl.program_id(0)
    kvext_ref[0:BLK, :] = halo_ref[...]
    kvext_ref[BLK:, :] = kv_ref[...]

    lane_kv = lax.broadcasted_iota(jnp.int32, (2 * BLK, LANES), 1) < HEAD_DIM
    lane_o = lax.broadcasted_iota(jnp.int32, (BLK, LANES), 1) < HEAD_DIM
    row = lax.broadcasted_iota(jnp.int32, (BLK, 2 * BLK), 0)
    col = lax.broadcasted_iota(jnp.int32, (BLK, 2 * BLK), 1)
    dist = row + BLK - col
    in_window = (dist >= 0) & (dist < WINDOW)
    distf = dist.astype(F32)
    nt_dims = (((1,), (1,)), ((), ()))

    def body(b, carry):
        r0 = pl.multiple_of(b * BLK, BLK)
        first = ((i * n_qb + b) % (SEQ // BLK) == 0).astype(jnp.int32)
        ok = in_window & (col >= BLK * first)
        neg = jnp.where(ok, 0.0, NEG_INF)

        band = kvext_ref[pl.ds(r0, 2 * BLK), :]
        kt = band[:, 0:LANES]
        vt = band[:, LANES:2 * LANES]
        kt_sw = pltpu.roll(kt, HEAD_DIM, 1)
        vt_sw = pltpu.roll(vt, HEAD_DIM, 1)
        zero = jnp.zeros_like(kt)
        one = jnp.ones_like(vt)
        qb = q_ref[pl.ds(r0, BLK), :]

        for kv in range(N_KV):
            k_here, k_other = (kt, kt_sw) if kv == 0 else (kt_sw, kt)
            v_here, v_other = (vt, vt_sw) if kv == 0 else (vt_sw, vt)
            k_rhs = (jnp.where(lane_kv, k_here, zero), jnp.where(lane_kv, zero, k_other))
            v_rhs = (jnp.where(lane_kv, v_here, one), jnp.where(lane_kv, one, v_other))
            base = kv * GQA * HEAD_DIM
            q4 = jnp.concatenate(
                [qb[:, base + p * LANES: base + (p + 1) * LANES] for p in range(GQA // 2)], axis=0)
            q4 = q4 * jnp.asarray(HEAD_DIM ** -0.5, BF16)

            res = []
            mx = []
            for parity in range(2):
                s = lax.dot_general(q4, k_rhs[parity], nt_dims, preferred_element_type=F32)
                probs = []
                mrow = []
                for p in range(GQA // 2):
                    h = kv * GQA + 2 * p + parity
                    slope = 2.0 ** (-8.0 * (h + 1) / N_HEADS)
                    sp = s[p * BLK:(p + 1) * BLK] - slope * distf + neg
                    m = jnp.maximum(jnp.max(sp, axis=1, keepdims=True), sink_ref[layer, h])
                    probs.append(jnp.exp(sp - m).astype(BF16))
                    mrow.append(m)
                pmat = jnp.concatenate(probs, axis=0)
                res.append(jnp.dot(pmat, v_rhs[parity], preferred_element_type=F32))
                mx.append(mrow)

            for p in range(GQA // 2):
                h_e = kv * GQA + 2 * p
                r_e = res[0][p * BLK:(p + 1) * BLK]
                r_o = res[1][p * BLK:(p + 1) * BLK]
                numer = jnp.where(lane_o, r_e, r_o)
                sums = pltpu.roll(jnp.where(lane_o, r_o, r_e), HEAD_DIM, 1)
                sink_e = jnp.exp(sink_ref[layer, h_e] - mx[0][p])
                sink_o = jnp.exp(sink_ref[layer, h_e + 1] - mx[1][p])
                denom = sums + jnp.where(lane_o, sink_e, sink_o)
                o_ref[pl.ds(r0, BLK), base + p * LANES: base + (p + 1) * LANES] = (numer / denom).astype(o_ref.dtype)
        return carry

    lax.fori_loop(0, n_qb, body, 0)


def _attention(a, sinks, layer):
    tq = ATTN_TQ
    q_blk = COL_Q // ATTN_WIDTH
    kv_blk = COL_K // (2 * KV_WIDTH)
    return pl.pallas_call(
        functools.partial(_attn_kernel, layer=layer, tq=tq),
        grid=(TOKENS // tq,),
        in_specs=[pl.BlockSpec(memory_space=pltpu.SMEM),
                  pl.BlockSpec((tq, ATTN_WIDTH), lambda i: (i, q_blk)),
                  pl.BlockSpec((tq, 2 * KV_WIDTH), lambda i: (i, kv_blk)),
                  pl.BlockSpec((BLK, 2 * KV_WIDTH), lambda i: (jnp.maximum(i * (tq // BLK) - 1, 0), kv_blk))],
        out_specs=pl.BlockSpec((tq, ATTN_WIDTH), lambda i: (i, 0)),
        out_shape=jax.ShapeDtypeStruct((TOKENS, ATTN_WIDTH), BF16),
        scratch_shapes=[pltpu.VMEM((BLK + tq, 2 * KV_WIDTH), BF16)],
        compiler_params=_params("arbitrary"),
        name="swa_attention",
    )(sinks, a, a, a)


POOL_HALO = 16
CONV_HALO = SUBLANES_F32
C_BLK = IN_PROJ_TN
C_CHUNK = 256


def _mix_kernel(u_ref, c0_ref, c1_ref, c2_ref, c3_ref, att_ref, g_ref, wgrp_ref, wpo_ref, wao_ref, wco_ref,
                scale_ref, wcm_ref, out_ref, uext_ref, zext_ref, pa_ref, cv_ref, *, tm):
    i = pl.program_id(0)
    seq_start = (i % (SEQ // tm)) == 0
    c_refs = (c0_ref, c1_ref, c2_ref, c3_ref)

    def conv_cols(col, width):
        blk, off = divmod(col, C_BLK)
        assert off + width <= C_BLK
        return c_refs[blk][:, off:off + width].astype(F32)

    @pl.when(seq_start)
    def _():
        uext_ref[0:POOL_HALO, :] = jnp.zeros((POOL_HALO, POOL_WIDTH), F32)
        zext_ref[0:CONV_HALO, :] = jnp.zeros((CONV_HALO, CONV_WIDTH), F32)

    @pl.when(jnp.logical_not(seq_start))
    def _():
        uext_ref[0:POOL_HALO, :] = uext_ref[tm:tm + POOL_HALO, :]
        zext_ref[0:CONV_HALO, :] = zext_ref[tm:tm + CONV_HALO, :]

    uext_ref[POOL_HALO:, :] = u_ref[...].astype(F32)

    pos = (i * tm) % SEQ + lax.broadcasted_iota(jnp.int32, (tm, 1), 0)
    for gi, w in enumerate(POOL_WINDOWS):
        cs = slice(gi * POOL_GROUP, (gi + 1) * POOL_GROUP)
        tok = uext_ref[POOL_HALO:POOL_HALO + tm, cs]
        acc = tok
        for s in range(1, w):
            acc = acc + uext_ref[POOL_HALO - s:POOL_HALO - s + tm, cs]
        cnt = jnp.minimum(pos + 1, w).astype(F32)
        p = acc / cnt - tok
        y = jnp.dot(p.astype(BF16), wgrp_ref[gi], preferred_element_type=F32) * scale_ref[:, cs]
        pa_ref[:, cs] = y.astype(BF16)

    for q in range(CONV_WIDTH // C_CHUNK):
        cs = slice(q * C_CHUNK, (q + 1) * C_CHUNK)
        zext_ref[CONV_HALO:, cs] = (conv_cols(CONV_WIDTH + q * C_CHUNK, C_CHUNK)
                                    * conv_cols(2 * CONV_WIDTH + q * C_CHUNK, C_CHUNK))
    for q in range(CONV_WIDTH // C_CHUNK):
        cs = slice(q * C_CHUNK, (q + 1) * C_CHUNK)
        y = (wcm_ref[0:1, cs] * zext_ref[CONV_HALO - 2:CONV_HALO - 2 + tm, cs]
             + wcm_ref[1:2, cs] * zext_ref[CONV_HALO - 1:CONV_HALO - 1 + tm, cs]
             + wcm_ref[2:3, cs] * zext_ref[CONV_HALO:CONV_HALO + tm, cs])
        cv_ref[:, cs] = (conv_cols(q * C_CHUNK, C_CHUNK) * y).astype(BF16)

    half = D_MODEL // 2
    for nc in range(2):
        cs = slice(nc * half, (nc + 1) * half)
        ya = jnp.dot(pa_ref[...], wpo_ref[:, cs], preferred_element_type=F32)
        yb = jnp.dot(att_ref[...], wao_ref[:, cs], preferred_element_type=F32)
        yc = jnp.dot(cv_ref[...], wco_ref[:, cs], preferred_element_type=F32)
        ga = g_ref[:, nc * half:(nc + 1) * half].astype(F32)
        gb = g_ref[:, D_MODEL + nc * half:D_MODEL + (nc + 1) * half].astype(F32)
        gc = g_ref[:, 2 * D_MODEL + nc * half:2 * D_MODEL + (nc + 1) * half].astype(F32)
        out_ref[:, cs] = (ga * ya + gb * yb + gc * yc).astype(out_ref.dtype)


def _mix(a, att, gates, wgrp, wpo, wao, wco, pool_scale3, w_conv_mix, layer):
    tm = MIX_TM
    c_blk0 = COL_CB // C_BLK
    assert COL_CB % C_BLK == 0 and (3 * CONV_WIDTH) % C_BLK == 0
    c_specs = [pl.BlockSpec((tm, C_BLK), functools.partial(lambda i, k: (i, c_blk0 + k), k=k)) for k in range(4)]
    return pl.pallas_call(
        functools.partial(_mix_kernel, tm=tm),
        grid=(TOKENS // tm,),
        in_specs=[pl.BlockSpec((tm, POOL_WIDTH), lambda i: (i, 0))] + c_specs + [
            pl.BlockSpec((tm, ATTN_WIDTH), lambda i: (i, 0)),
            pl.BlockSpec((tm, N_GATE), lambda i: (i, 0)),
            _resident((None, len(POOL_WINDOWS), POOL_GROUP, POOL_GROUP), lambda i: (layer, 0, 0, 0)),
            _resident((None, POOL_WIDTH, D_MODEL), lambda i: (layer, 0, 0)),
            _resident((None, ATTN_WIDTH, D_MODEL), lambda i: (layer, 0, 0)),
            _resident((None, CONV_WIDTH, D_MODEL), lambda i: (layer, 0, 0)),
            _resident((None, 1, POOL_WIDTH), lambda i: (layer, 0, 0)),
            _resident((None, 3, CONV_WIDTH), lambda i: (layer, 0, 0)),
        ],
        out_specs=pl.BlockSpec((tm, D_MODEL), lambda i: (i, 0)),
        out_shape=jax.ShapeDtypeStruct((TOKENS, D_MODEL), BF16),
        scratch_shapes=[pltpu.VMEM((POOL_HALO + tm, POOL_WIDTH), F32),
                        pltpu.VMEM((CONV_HALO + tm, CONV_WIDTH), F32),
                        pltpu.VMEM((tm, POOL_WIDTH), BF16),
                        pltpu.VMEM((tm, CONV_WIDTH), BF16)],
        compiler_params=_params("arbitrary"),
        name="mixers_merge",
    )(a, a, a, a, a, att, gates, wgrp, wpo, wao, wco, pool_scale3, w_conv_mix)


def _proj_residual_kernel(a_ref, x_ref, w_ref, g_ref, *out_refs, final):
    x_new = x_ref[...] + jnp.dot(a_ref[...], w_ref[...], preferred_element_type=F32)
    normed = _rms_scale(x_new, g_ref[...])
    if final:
        out_refs[0][...] = normed
    else:
        out_refs[0][...] = x_new
        out_refs[1][...] = normed.astype(out_refs[1].dtype)


def _proj_residual(act, x, w, g3, w_layer, g_layer, tm, final, name):
    k = act.shape[1]
    tok_spec = pl.BlockSpec((tm, D_MODEL), lambda i: (i, 0))
    if final:
        out_specs = tok_spec
        out_shape = jax.ShapeDtypeStruct((TOKENS, D_MODEL), F32)
    else:
        out_specs = [tok_spec, tok_spec]
        out_shape = [jax.ShapeDtypeStruct((TOKENS, D_MODEL), F32), jax.ShapeDtypeStruct((TOKENS, D_MODEL), BF16)]
    g_spec = (_resident((1, D_MODEL), lambda i: (0, 0)) if g_layer is None
              else _resident((None, 1, D_MODEL), lambda i: (g_layer, 0, 0)))
    return pl.pallas_call(
        functools.partial(_proj_residual_kernel, final=final),
        grid=(TOKENS // tm,),
        in_specs=[pl.BlockSpec((tm, k), lambda i: (i, 0)),
                  tok_spec,
                  _resident((None, k, D_MODEL), lambda i: (w_layer, 0, 0)),
                  g_spec],
        out_specs=out_specs,
        out_shape=out_shape,
        compiler_params=_params("arbitrary"),
        name=name,
    )(act, x, w, g3)


def _up_kernel(h_ref, wg_ref, wv_ref, cwg_ref, cwv_ref, bg_ref, bv_ref, o_ref, wbf_ref, zext_ref, *, tm, tn):
    m = pl.program_id(1)

    @pl.when(m == 0)
    def _():
        wbf_ref[:, 0:tn] = wg_ref[...].astype(BF16)
        wbf_ref[:, tn:2 * tn] = wv_ref[...].astype(BF16)

    seq_start = (m % (SEQ // tm)) == 0

    @pl.when(seq_start)
    def _():
        zext_ref[0:CONV_HALO, :] = jnp.zeros((CONV_HALO, 2 * tn), F32)

    @pl.when(jnp.logical_not(seq_start))
    def _():
        zext_ref[0:CONV_HALO, :] = zext_ref[tm:tm + CONV_HALO, :]

    zext_ref[CONV_HALO:, :] = jnp.dot(h_ref[...], wbf_ref[...], preferred_element_type=F32)

    def conv(cs, cw_ref, b_ref):
        return (cw_ref[0:1, :] * zext_ref[CONV_HALO - 2:CONV_HALO - 2 + tm, cs]
                + cw_ref[1:2, :] * zext_ref[CONV_HALO - 1:CONV_HALO - 1 + tm, cs]
                + cw_ref[2:3, :] * zext_ref[CONV_HALO:CONV_HALO + tm, cs]
                + b_ref[...])

    u_gate = conv(slice(0, tn), cwg_ref, bg_ref)
    u_val = conv(slice(tn, 2 * tn), cwv_ref, bv_ref)
    silu = u_gate / (1.0 + jnp.exp(-u_gate))
    o_ref[...] = (silu * u_val).astype(o_ref.dtype)


def _up(h2, w_up, w_ffn_conv, b_ffn_conv3, layer):
    tm, tn = UP_TM, UP_TN
    nj = D_FF // tn
    return pl.pallas_call(
        functools.partial(_up_kernel, tm=tm, tn=tn),
        grid=(nj, TOKENS // tm),
        in_specs=[pl.BlockSpec((tm, D_MODEL), lambda j, m: (m, 0)),
                  pl.BlockSpec((None, D_MODEL, tn), lambda j, m: (layer, 0, j)),
                  pl.BlockSpec((None, D_MODEL, tn), lambda j, m: (layer, 0, j + nj)),
                  pl.BlockSpec((None, 3, tn), lambda j, m: (layer, 0, j)),
                  pl.BlockSpec((None, 3, tn), lambda j, m: (layer, 0, j + nj)),
                  pl.BlockSpec((None, 1, tn), lambda j, m: (layer, 0, j)),
                  pl.BlockSpec((None, 1, tn), lambda j, m: (layer, 0, j + nj))],
        out_specs=pl.BlockSpec((tm, tn), lambda j, m: (m, j)),
        out_shape=jax.ShapeDtypeStruct((TOKENS, D_FF), BF16),
        scratch_shapes=[pltpu.VMEM((D_MODEL, 2 * tn), BF16),
                        pltpu.VMEM((CONV_HALO + tm, 2 * tn), F32)],
        compiler_params=_params("arbitrary", "arbitrary"),
        name="up_conv_act",
    )(h2, w_up, w_up, w_ffn_conv, w_ffn_conv, b_ffn_conv3, b_ffn_conv3)


def kernel(x, g_mix, w_in, b_gate, w_pool_grp, pool_scale, w_pool_out, sinks, w_attn_out, w_conv_mix, w_conv_out,
           w_o, g_ffn, w_up, w_ffn_conv, b_ffn_conv, w_down, g_final):
    assert x.shape == (BATCH, SEQ, D_MODEL) and x.dtype == F32
    xt = x.reshape(TOKENS, D_MODEL)
    g_mix3 = g_mix.reshape(DEPTH, 1, D_MODEL)
    g_ffn3 = g_ffn.reshape(DEPTH, 1, D_MODEL)
    g_final2 = g_final.reshape(1, D_MODEL)
    b_gate3 = b_gate.reshape(DEPTH, 1, N_GATE)
    pool_scale3 = pool_scale.reshape(DEPTH, 1, POOL_WIDTH)
    b_ffn3 = b_ffn_conv.reshape(DEPTH, 1, 2 * D_FF)
    wgrp, wpo, wao, wco, wo, wdn = (w.astype(BF16) for w in (w_pool_grp, w_pool_out, w_attn_out, w_conv_out, w_o, w_down))

    h = _rmsnorm_bf16(xt, g_mix3, 0)
    for layer in range(DEPTH):
        a = _in_proj(h, w_in, None, layer, gates=False)
        gates = _in_proj(h, w_in, b_gate3, layer, gates=True)
        att = _attention(a, sinks, layer)
        merged = _mix(a, att, gates, wgrp, wpo, wao, wco, pool_scale3, w_conv_mix, layer)
        xt, h2 = _proj_residual(merged, xt, wo, g_ffn3, layer, layer, OPROJ_TM, False, "out_proj_norm")
        act = _up(h2, w_up, w_ffn_conv, b_ffn3, layer)
        if layer + 1 < DEPTH:
            xt, h = _proj_residual(act, xt, wdn, g_mix3, layer, layer + 1, DOWN_TM, False, "down_proj_norm")
        else:
            out = _proj_residual(act, xt, wdn, g_final2, layer, None, DOWN_TM, True, "down_proj_final")
    return out.reshape(BATCH, SEQ, D_MODEL)
```

```python
import functools

import jax
import jax.numpy as jnp
from jax import lax
from jax.experimental import pallas as pl
from jax.experimental.pallas import tpu as pltpu

D_MODEL = 2048
BATCH = 2
SEQ = 16384
DEPTH = 2
TOKENS = BATCH * SEQ

POOL_WIDTH = D_MODEL // 2
POOL_WINDOWS = (2, 4, 8, 16)
POOL_GROUP = POOL_WIDTH // len(POOL_WINDOWS)
HEAD_DIM = 64
N_HEADS = D_MODEL // 128
GQA = 8
N_KV = N_HEADS // GQA
WINDOW = 128
BLK = 128
CONV_WIDTH = D_MODEL // 2
D_FF = ((8 * D_MODEL // 3 + 255) // 256) * 256
N_BRANCH = 3
RMS_EPS = 1e-6
NEG_INF = -1e30

ATTN_WIDTH = N_HEADS * HEAD_DIM
KV_WIDTH = N_KV * HEAD_DIM
COL_U = 0
COL_Q = COL_U + POOL_WIDTH
COL_K = COL_Q + ATTN_WIDTH
COL_V = COL_K + KV_WIDTH
COL_CB = COL_V + KV_WIDTH
COL_CC = COL_CB + CONV_WIDTH
COL_CH = COL_CC + CONV_WIDTH
COL_GATE = COL_CH + CONV_WIDTH
N_MIX_IN = COL_GATE
N_GATE = N_BRANCH * D_MODEL

LANES = 128
SUBLANES_F32 = 8
SUBLANES_BF16 = 16
VMEM_LIMIT_BYTES = 56 * 1024 * 1024

BF16 = jnp.bfloat16
F32 = jnp.float32

IN_PROJ_TN = 768
IN_PROJ_TM = 2048
ATTN_TQ = 512
MIX_TM = 512
OPROJ_TM = 512
UP_TM = 512
UP_TN = 1408
DOWN_TM = 256
NORM_TM = 1024


def _params(*sem):
    return pltpu.CompilerParams(dimension_semantics=sem, vmem_limit_bytes=VMEM_LIMIT_BYTES)


def _resident(shape, index_map):
    return pl.BlockSpec(shape, index_map, pipeline_mode=pl.Buffered(1))


def _sigmoid(z):
    return 0.5 * jnp.tanh(0.5 * z) + 0.5


def _rms_scale(x, g):
    ms = jnp.mean(x * x, axis=-1, keepdims=True)
    return x * lax.rsqrt(ms + RMS_EPS) * g


def _rmsnorm_kernel(x_ref, g_ref, o_ref):
    o_ref[...] = _rms_scale(x_ref[...], g_ref[...]).astype(o_ref.dtype)


def _rmsnorm_bf16(x, g3, layer):
    tm = NORM_TM
    return pl.pallas_call(
        _rmsnorm_kernel,
        grid=(TOKENS // tm,),
        in_specs=[pl.BlockSpec((tm, D_MODEL), lambda i: (i, 0)),
                  pl.BlockSpec((None, 1, D_MODEL), lambda i: (layer, 0, 0))],
        out_specs=pl.BlockSpec((tm, D_MODEL), lambda i: (i, 0)),
        out_shape=jax.ShapeDtypeStruct((TOKENS, D_MODEL), BF16),
        compiler_params=_params("arbitrary"),
        name="rmsnorm_in",
    )(x, g3)


def _in_proj_kernel(h_ref, w_ref, o_ref, wbf_ref):
    @pl.when(pl.program_id(1) == 0)
    def _():
        wbf_ref[...] = w_ref[...].astype(BF16)

    acc = jnp.dot(h_ref[...], wbf_ref[...], preferred_element_type=F32)
    o_ref[...] = acc.astype(o_ref.dtype)


def _in_proj_gate_kernel(h_ref, w_ref, b_ref, o_ref, wbf_ref):
    @pl.when(pl.program_id(1) == 0)
    def _():
        wbf_ref[...] = w_ref[...].astype(BF16)

    z = jnp.dot(h_ref[...], wbf_ref[...], preferred_element_type=F32) + b_ref[...]
    o_ref[...] = _sigmoid(z).astype(o_ref.dtype)


def _in_proj(h, w_in, b_gate3, layer, gates):
    tm, tn = IN_PROJ_TM, IN_PROJ_TN
    n_out = N_GATE if gates else N_MIX_IN
    col0 = (COL_GATE if gates else 0) // tn
    in_specs = [pl.BlockSpec((tm, D_MODEL), lambda j, m: (m, 0)),
                pl.BlockSpec((None, D_MODEL, tn), lambda j, m: (layer, 0, j + col0))]
    args = [h, w_in]
    if gates:
        in_specs.append(pl.BlockSpec((None, 1, tn), lambda j, m: (layer, 0, j)))
        args.append(b_gate3)
    return pl.pallas_call(
        _in_proj_gate_kernel if gates else _in_proj_kernel,
        grid=(n_out // tn, TOKENS // tm),
        in_specs=in_specs,
        out_specs=pl.BlockSpec((tm, tn), lambda j, m: (m, j)),
        out_shape=jax.ShapeDtypeStruct((TOKENS, n_out), BF16),
        scratch_shapes=[pltpu.VMEM((D_MODEL, tn), BF16)],
        compiler_params=_params("arbitrary", "arbitrary"),
        name="in_proj_gates" if gates else "in_proj_mix",
    )(*args)


def _attn_kernel(sink_ref, q_ref, kv_ref, halo_ref, o_ref, kvext_ref, *, layer, tq):
    n_qb = tq // BLK
    i = pl.program_id(0)
    kvext_ref[0:BLK, :] = halo_ref[...]
    kvext_ref[BLK:, :] = kv_ref[...]

    lane_kv = lax.broadcasted_iota(jnp.int32, (2 * BLK, LANES), 1) < HEAD_DIM
    lane_o = lax.broadcasted_iota(jnp.int32, (BLK, LANES), 1) < HEAD_DIM
    row = lax.broadcasted_iota(jnp.int32, (BLK, 2 * BLK), 0)
    col = lax.broadcasted_iota(jnp.int32, (BLK, 2 * BLK), 1)
    dist = row + BLK - col
    in_window = (dist >= 0) & (dist < WINDOW)
    distf = dist.astype(F32)
    nt_dims = (((1,), (1,)), ((), ()))

    def body(b, carry):
        r0 = pl.multiple_of(b * BLK, BLK)
        first = ((i * n_qb + b) % (SEQ // BLK) == 0).astype(jnp.int32)
        ok = in_window & (col >= BLK * first)
        neg = jnp.where(ok, 0.0, NEG_INF)

        band = kvext_ref[pl.ds(r0, 2 * BLK), :]
        kt = band[:, 0:LANES]
        vt = band[:, LANES:2 * LANES]
        kt_sw = pltpu.roll(kt, HEAD_DIM, 1)
        vt_sw = pltpu.roll(vt, HEAD_DIM, 1)
        zero = jnp.zeros_like(kt)
        one = jnp.ones_like(vt)
        qb = q_ref[pl.ds(r0, BLK), :]

        for kv in range(N_KV):
            k_here, k_other = (kt, kt_sw) if kv == 0 else (kt_sw, kt)
            v_here, v_other = (vt, vt_sw) if kv == 0 else (vt_sw, vt)
            k_rhs = (jnp.where(lane_kv, k_here, zero), jnp.where(lane_kv, zero, k_other))
            v_rhs = (jnp.where(lane_kv, v_here, one), jnp.where(lane_kv, one, v_other))
            base = kv * GQA * HEAD_DIM
            q4 = jnp.concatenate(
                [qb[:, base + p * LANES: base + (p + 1) * LANES] for p in range(GQA // 2)], axis=0)
            q4 = q4 * jnp.asarray(HEAD_DIM ** -0.5, BF16)

            res = []
            mx = []
            for parity in range(2):
                s = lax.dot_general(q4, k_rhs[parity], nt_dims, preferred_element_type=F32)
                probs = []
                mrow = []
                for p in range(GQA // 2):
                    h = kv * GQA + 2 * p + parity
                    slope = 2.0 ** (-8.0 * (h + 1) / N_HEADS)
                    sp = s[p * BLK:(p + 1) * BLK] - slope * distf + neg
                    m = jnp.maximum(jnp.max(sp, axis=1, keepdims=True), sink_ref[layer, h])
                    probs.append(jnp.exp(sp - m).astype(BF16))
                    mrow.append(m)
                pmat = jnp.concatenate(probs, axis=0)
                res.append(jnp.dot(pmat, v_rhs[parity], preferred_element_type=F32))
                mx.append(mrow)

            for p in range(GQA // 2):
                h_e = kv * GQA + 2 * p
                r_e = res[0][p * BLK:(p + 1) * BLK]
                r_o = res[1][p * BLK:(p + 1) * BLK]
                numer = jnp.where(lane_o, r_e, r_o)
                sums = pltpu.roll(jnp.where(lane_o, r_o, r_e), HEAD_DIM, 1)
                sink_e = jnp.exp(sink_ref[layer, h_e] - mx[0][p])
                sink_o = jnp.exp(sink_ref[layer, h_e + 1] - mx[1][p])
                denom = sums + jnp.where(lane_o, sink_e, sink_o)
                o_ref[pl.ds(r0, BLK), base + p * LANES: base + (p + 1) * LANES] = (numer / denom).astype(o_ref.dtype)
        return carry

    lax.fori_loop(0, n_qb, body, 0)


def _attention(a, sinks, layer):
    tq = ATTN_TQ
    q_blk = COL_Q // ATTN_WIDTH
    kv_blk = COL_K // (2 * KV_WIDTH)
    return pl.pallas_call(
        functools.partial(_attn_kernel, layer=layer, tq=tq),
        grid=(TOKENS // tq,),
        in_specs=[pl.BlockSpec(memory_space=pltpu.SMEM),
                  pl.BlockSpec((tq, ATTN_WIDTH), lambda i: (i, q_blk)),
                  pl.BlockSpec((tq, 2 * KV_WIDTH), lambda i: (i, kv_blk)),
                  pl.BlockSpec((BLK, 2 * KV_WIDTH), lambda i: (jnp.maximum(i * (tq // BLK) - 1, 0), kv_blk))],
        out_specs=pl.BlockSpec((tq, ATTN_WIDTH), lambda i: (i, 0)),
        out_shape=jax.ShapeDtypeStruct((TOKENS, ATTN_WIDTH), BF16),
        scratch_shapes=[pltpu.VMEM((BLK + tq, 2 * KV_WIDTH), BF16)],
        compiler_params=_params("arbitrary"),
        name="swa_attention",
    )(sinks, a, a, a)


POOL_HALO = 16
CONV_HALO = SUBLANES_F32
C_BLK = IN_PROJ_TN
C_CHUNK = 256


def _mix_kernel(u_ref, c0_ref, c1_ref, c2_ref, c3_ref, att_ref, g_ref, wgrp_ref, wpo_ref, wao_ref, wco_ref,
                scale_ref, wcm_ref, out_ref, uext_ref, zext_ref, pa_ref, cv_ref, *, tm):
    i = pl.program_id(0)
    seq_start = (i % (SEQ // tm)) == 0
    c_refs = (c0_ref, c1_ref, c2_ref, c3_ref)

    def conv_cols(col, width):
        blk, off = divmod(col, C_BLK)
        assert off + width <= C_BLK
        return c_refs[blk][:, off:off + width].astype(F32)

    @pl.when(seq_start)
    def _():
        uext_ref[0:POOL_HALO, :] = jnp.zeros((POOL_HALO, POOL_WIDTH), F32)
        zext_ref[0:CONV_HALO, :] = jnp.zeros((CONV_HALO, CONV_WIDTH), F32)

    @pl.when(jnp.logical_not(seq_start))
    def _():
        uext_ref[0:POOL_HALO, :] = uext_ref[tm:tm + POOL_HALO, :]
        zext_ref[0:CONV_HALO, :] = zext_ref[tm:tm + CONV_HALO, :]

    uext_ref[POOL_HALO:, :] = u_ref[...].astype(F32)

    pos = (i * tm) % SEQ + lax.broadcasted_iota(jnp.int32, (tm, 1), 0)
    for gi, w in enumerate(POOL_WINDOWS):
        cs = slice(gi * POOL_GROUP, (gi + 1) * POOL_GROUP)
        tok = uext_ref[POOL_HALO:POOL_HALO + tm, cs]
        acc = tok
        for s in range(1, w):
            acc = acc + uext_ref[POOL_HALO - s:POOL_HALO - s + tm, cs]
        cnt = jnp.minimum(pos + 1, w).astype(F32)
        p = acc / cnt - tok
        y = jnp.dot(p.astype(BF16), wgrp_ref[gi], preferred_element_type=F32) * scale_ref[:, cs]
        pa_ref[:, cs] = y.astype(BF16)

    for q in range(CONV_WIDTH // C_CHUNK):
        cs = slice(q * C_CHUNK, (q + 1) * C_CHUNK)
        zext_ref[CONV_HALO:, cs] = (conv_cols(CONV_WIDTH + q * C_CHUNK, C_CHUNK)
                                    * conv_cols(2 * CONV_WIDTH + q * C_CHUNK, C_CHUNK))
    for q in range(CONV_WIDTH // C_CHUNK):
        cs = slice(q * C_CHUNK, (q + 1) * C_CHUNK)
        y = (wcm_ref[0:1, cs] * zext_ref[CONV_HALO - 2:CONV_HALO - 2 + tm, cs]
             + wcm_ref[1:2, cs] * zext_ref[CONV_HALO - 1:CONV_HALO - 1 + tm, cs]
             + wcm_ref[2:3, cs] * zext_ref[CONV_HALO:CONV_HALO + tm, cs])
        cv_ref[:, cs] = (conv_cols(q * C_CHUNK, C_CHUNK) * y).astype(BF16)

    half = D_MODEL // 2
    for nc in range(2):
        cs = slice(nc * half, (nc + 1) * half)
        ya = jnp.dot(pa_ref[...], wpo_ref[:, cs], preferred_element_type=F32)
        yb = jnp.dot(att_ref[...], wao_ref[:, cs], preferred_element_type=F32)
        yc = jnp.dot(cv_ref[...], wco_ref[:, cs], preferred_element_type=F32)
        ga = g_ref[:, nc * half:(nc + 1) * half].astype(F32)
        gb = g_ref[:, D_MODEL + nc * half:D_MODEL + (nc + 1) * half].astype(F32)
        gc = g_ref[:, 2 * D_MODEL + nc * half:2 * D_MODEL + (nc + 1) * half].astype(F32)
        out_ref[:, cs] = (ga * ya + gb * yb + gc * yc).astype(out_ref.dtype)


def _mix(a, att, gates, wgrp, wpo, wao, wco, pool_scale3, w_conv_mix, layer):
    tm = MIX_TM
    c_blk0 = COL_CB // C_BLK
    assert COL_CB % C_BLK == 0 and (3 * CONV_WIDTH) % C_BLK == 0
    c_specs = [pl.BlockSpec((tm, C_BLK), functools.partial(lambda i, k: (i, c_blk0 + k), k=k)) for k in range(4)]
    return pl.pallas_call(
        functools.partial(_mix_kernel, tm=tm),
        grid=(TOKENS // tm,),
        in_specs=[pl.BlockSpec((tm, POOL_WIDTH), lambda i: (i, 0))] + c_specs + [
            pl.BlockSpec((tm, ATTN_WIDTH), lambda i: (i, 0)),
            pl.BlockSpec((tm, N_GATE), lambda i: (i, 0)),
            _resident((None, len(POOL_WINDOWS), POOL_GROUP, POOL_GROUP), lambda i: (layer, 0, 0, 0)),
            _resident((None, POOL_WIDTH, D_MODEL), lambda i: (layer, 0, 0)),
            _resident((None, ATTN_WIDTH, D_MODEL), lambda i: (layer, 0, 0)),
            _resident((None, CONV_WIDTH, D_MODEL), lambda i: (layer, 0, 0)),
            _resident((None, 1, POOL_WIDTH), lambda i: (layer, 0, 0)),
            _resident((None, 3, CONV_WIDTH), lambda i: (layer, 0, 0)),
        ],
        out_specs=pl.BlockSpec((tm, D_MODEL), lambda i: (i, 0)),
        out_shape=jax.ShapeDtypeStruct((TOKENS, D_MODEL), BF16),
        scratch_shapes=[pltpu.VMEM((POOL_HALO + tm, POOL_WIDTH), F32),
                        pltpu.VMEM((CONV_HALO + tm, CONV_WIDTH), F32),
                        pltpu.VMEM((tm, POOL_WIDTH), BF16),
                        pltpu.VMEM((tm, CONV_WIDTH), BF16)],
        compiler_params=_params("arbitrary"),
        name="mixers_merge",
    )(a, a, a, a, a, att, gates, wgrp, wpo, wao, wco, pool_scale3, w_conv_mix)


def _proj_residual_kernel(a_ref, x_ref, w_ref, g_ref, *out_refs, final):
    x_new = x_ref[...] + jnp.dot(a_ref[...], w_ref[...], preferred_element_type=F32)
    normed = _rms_scale(x_new, g_ref[...])
    if final:
        out_refs[0][...] = normed
    else:
        out_refs[0][...] = x_new
        out_refs[1][...] = normed.astype(out_refs[1].dtype)


def _proj_residual(act, x, w, g3, w_layer, g_layer, tm, final, name):
    k = act.shape[1]
    tok_spec = pl.BlockSpec((tm, D_MODEL), lambda i: (i, 0))
    if final:
        out_specs = tok_spec
        out_shape = jax.ShapeDtypeStruct((TOKENS, D_MODEL), F32)
    else:
        out_specs = [tok_spec, tok_spec]
        out_shape = [jax.ShapeDtypeStruct((TOKENS, D_MODEL), F32), jax.ShapeDtypeStruct((TOKENS, D_MODEL), BF16)]
    g_spec = (_resident((1, D_MODEL), lambda i: (0, 0)) if g_layer is None
              else _resident((None, 1, D_MODEL), lambda i: (g_layer, 0, 0)))
    return pl.pallas_call(
        functools.partial(_proj_residual_kernel, final=final),
        grid=(TOKENS // tm,),
        in_specs=[pl.BlockSpec((tm, k), lambda i: (i, 0)),
                  tok_spec,
                  _resident((None, k, D_MODEL), lambda i: (w_layer, 0, 0)),
                  g_spec],
        out_specs=out_specs,
        out_shape=out_shape,
        compiler_params=_params("arbitrary"),
        name=name,
    )(act, x, w, g3)


def _up_kernel(h_ref, wg_ref, wv_ref, cwg_ref, cwv_ref, bg_ref, bv_ref, o_ref, wbf_ref, zext_ref, *, tm, tn):
    m = pl.program_id(1)

    @pl.when(m == 0)
    def _():
        wbf_ref[:, 0:tn] = wg_ref[...].astype(BF16)
        wbf_ref[:, tn:2 * tn] = wv_ref[...].astype(BF16)

    seq_start = (m % (SEQ // tm)) == 0

    @pl.when(seq_start)
    def _():
        zext_ref[0:CONV_HALO, :] = jnp.zeros((CONV_HALO, 2 * tn), F32)

    @pl.when(jnp.logical_not(seq_start))
    def _():
        zext_ref[0:CONV_HALO, :] = zext_ref[tm:tm + CONV_HALO, :]

    zext_ref[CONV_HALO:, :] = jnp.dot(h_ref[...], wbf_ref[...], preferred_element_type=F32)

    def conv(cs, cw_ref, b_ref, scale):
        taps = cw_ref[...] * scale
        return (taps[0:1] * zext_ref[CONV_HALO - 2:CONV_HALO - 2 + tm, cs]
                + taps[1:2] * zext_ref[CONV_HALO - 1:CONV_HALO - 1 + tm, cs]
                + taps[2:3] * zext_ref[CONV_HALO:CONV_HALO + tm, cs]
                + b_ref[...] * scale)

    half_gate = conv(slice(0, tn), cwg_ref, bg_ref, 0.5)
    u_val = conv(slice(tn, 2 * tn), cwv_ref, bv_ref, 1.0)
    o_ref[...] = (half_gate * (1.0 + jnp.tanh(half_gate)) * u_val).astype(o_ref.dtype)


def _up(h2, w_up, w_ffn_conv, b_ffn_conv3, layer):
    tm, tn = UP_TM, UP_TN
    nj = D_FF // tn
    w_spec = lambda off: pl.BlockSpec((None, D_MODEL, tn), lambda j, m: (layer, 0, j + off),
                                      pipeline_mode=pl.Buffered(1))
    return pl.pallas_call(
        functools.partial(_up_kernel, tm=tm, tn=tn),
        grid=(nj, TOKENS // tm),
        in_specs=[pl.BlockSpec((tm, D_MODEL), lambda j, m: (m, 0)),
                  w_spec(0),
                  w_spec(nj),
                  pl.BlockSpec((None, 3, tn), lambda j, m: (layer, 0, j)),
                  pl.BlockSpec((None, 3, tn), lambda j, m: (layer, 0, j + nj)),
                  pl.BlockSpec((None, 1, tn), lambda j, m: (layer, 0, j)),
                  pl.BlockSpec((None, 1, tn), lambda j, m: (layer, 0, j + nj))],
        out_specs=pl.BlockSpec((tm, tn), lambda j, m: (m, j)),
        out_shape=jax.ShapeDtypeStruct((TOKENS, D_FF), BF16),
        scratch_shapes=[pltpu.VMEM((D_MODEL, 2 * tn), BF16),
                        pltpu.VMEM((CONV_HALO + tm, 2 * tn), F32)],
        compiler_params=_params("arbitrary", "arbitrary"),
        name="up_conv_act",
    )(h2, w_up, w_up, w_ffn_conv, w_ffn_conv, b_ffn_conv3, b_ffn_conv3)


def kernel(x, g_mix, w_in, b_gate, w_pool_grp, pool_scale, w_pool_out, sinks, w_attn_out, w_conv_mix, w_conv_out,
           w_o, g_ffn, w_up, w_ffn_conv, b_ffn_conv, w_down, g_final):
    assert x.shape == (BATCH, SEQ, D_MODEL) and x.dtype == F32
    xt = x.reshape(TOKENS, D_MODEL)
    g_mix3 = g_mix.reshape(DEPTH, 1, D_MODEL)
    g_ffn3 = g_ffn.reshape(DEPTH, 1, D_MODEL)
    g_final2 = g_final.reshape(1, D_MODEL)
    b_gate3 = b_gate.reshape(DEPTH, 1, N_GATE)
    pool_scale3 = pool_scale.reshape(DEPTH, 1, POOL_WIDTH)
    b_ffn3 = b_ffn_conv.reshape(DEPTH, 1, 2 * D_FF)
    wgrp, wpo, wao, wco, wo, wdn = (w.astype(BF16) for w in (w_pool_grp, w_pool_out, w_attn_out, w_conv_out, w_o, w_down))

    h = _rmsnorm_bf16(xt, g_mix3, 0)
    for layer in range(DEPTH):
        a = _in_proj(h, w_in, None, layer, gates=False)
        gates = _in_proj(h, w_in, b_gate3, layer, gates=True)
        att = _attention(a, sinks, layer)
        merged = _mix(a, att, gates, wgrp, wpo, wao, wco, pool_scale3, w_conv_mix, layer)
        xt, h2 = _proj_residual(merged, xt, wo, g_ffn3, layer, layer, OPROJ_TM, False, "out_proj_norm")
        act = _up(h2, w_up, w_ffn_conv, b_ffn3, layer)
        if layer + 1 < DEPTH:
            xt, h = _proj_residual(act, xt, wdn, g_mix3, layer, layer + 1, DOWN_TM, False, "down_proj_norm")
        else:
            out = _proj_residual(act, xt, wdn, g_final2, layer, None, DOWN_TM, True, "down_proj_final")
    return out.reshape(BATCH, SEQ, D_MODEL)
```

```python
import functools

import jax
import jax.numpy as jnp
from jax import lax
from jax.experimental import pallas as pl
from jax.experimental.pallas import tpu as pltpu

D_MODEL = 2048
BATCH = 2
SEQ = 16384
DEPTH = 2
TOKENS = BATCH * SEQ

POOL_WIDTH = D_MODEL // 2
POOL_WINDOWS = (2, 4, 8, 16)
POOL_GROUP = POOL_WIDTH // len(POOL_WINDOWS)
HEAD_DIM = 64
N_HEADS = D_MODEL // 128
GQA = 8
N_KV = N_HEADS // GQA
WINDOW = 128
BLK = 128
CONV_WIDTH = D_MODEL // 2
D_FF = ((8 * D_MODEL // 3 + 255) // 256) * 256
N_BRANCH = 3
RMS_EPS = 1e-6
NEG_INF = -1e30

ATTN_WIDTH = N_HEADS * HEAD_DIM
KV_WIDTH = N_KV * HEAD_DIM
COL_U = 0
COL_Q = COL_U + POOL_WIDTH
COL_K = COL_Q + ATTN_WIDTH
COL_V = COL_K + KV_WIDTH
COL_CB = COL_V + KV_WIDTH
COL_CC = COL_CB + CONV_WIDTH
COL_CH = COL_CC + CONV_WIDTH
COL_GATE = COL_CH + CONV_WIDTH
N_MIX_IN = COL_GATE
N_GATE = N_BRANCH * D_MODEL

LANES = 128
SUBLANES_F32 = 8
SUBLANES_BF16 = 16
VMEM_LIMIT_BYTES = 56 * 1024 * 1024

BF16 = jnp.bfloat16
F32 = jnp.float32

IN_PROJ_TN = 768
IN_PROJ_TM = 2048
ATTN_TQ = 512
MIX_TM = 512
OPROJ_TM = 512
UP_TM = 512
UP_TN = 1408
DOWN_TM = 256
NORM_TM = 1024


def _params(*sem):
    return pltpu.CompilerParams(dimension_semantics=sem, vmem_limit_bytes=VMEM_LIMIT_BYTES)


def _resident(shape, index_map):
    return pl.BlockSpec(shape, index_map, pipeline_mode=pl.Buffered(1))


def _sigmoid(z):
    return 0.5 * jnp.tanh(0.5 * z) + 0.5


def _rms_scale(x, g):
    ms = jnp.mean(x * x, axis=-1, keepdims=True)
    return x * lax.rsqrt(ms + RMS_EPS) * g


def _rmsnorm_kernel(x_ref, g_ref, o_ref):
    o_ref[...] = _rms_scale(x_ref[...], g_ref[...]).astype(o_ref.dtype)


def _rmsnorm_bf16(x, g3, layer):
    tm = NORM_TM
    return pl.pallas_call(
        _rmsnorm_kernel,
        grid=(TOKENS // tm,),
        in_specs=[pl.BlockSpec((tm, D_MODEL), lambda i: (i, 0)),
                  pl.BlockSpec((None, 1, D_MODEL), lambda i: (layer, 0, 0))],
        out_specs=pl.BlockSpec((tm, D_MODEL), lambda i: (i, 0)),
        out_shape=jax.ShapeDtypeStruct((TOKENS, D_MODEL), BF16),
        compiler_params=_params("arbitrary"),
        name="rmsnorm_in",
    )(x, g3)


def _in_proj_kernel(h_ref, w_ref, o_ref, wbf_ref):
    @pl.when(pl.program_id(1) == 0)
    def _():
        wbf_ref[...] = w_ref[...].astype(BF16)

    acc = jnp.dot(h_ref[...], wbf_ref[...], preferred_element_type=F32)
    o_ref[...] = acc.astype(o_ref.dtype)


def _in_proj_gate_kernel(h_ref, w_ref, b_ref, o_ref, wbf_ref):
    @pl.when(pl.program_id(1) == 0)
    def _():
        wbf_ref[...] = w_ref[...].astype(BF16)

    z = jnp.dot(h_ref[...], wbf_ref[...], preferred_element_type=F32) + b_ref[...]
    o_ref[...] = _sigmoid(z).astype(o_ref.dtype)


def _in_proj(h, w_in, b_gate3, layer, gates):
    tm, tn = IN_PROJ_TM, IN_PROJ_TN
    n_out = N_GATE if gates else N_MIX_IN
    col0 = (COL_GATE if gates else 0) // tn
    in_specs = [pl.BlockSpec((tm, D_MODEL), lambda j, m: (m, 0)),
                pl.BlockSpec((None, D_MODEL, tn), lambda j, m: (layer, 0, j + col0))]
    args = [h, w_in]
    if gates:
        in_specs.append(pl.BlockSpec((None, 1, tn), lambda j, m: (layer, 0, j)))
        args.append(b_gate3)
    return pl.pallas_call(
        _in_proj_gate_kernel if gates else _in_proj_kernel,
        grid=(n_out // tn, TOKENS // tm),
        in_specs=in_specs,
        out_specs=pl.BlockSpec((tm, tn), lambda j, m: (m, j)),
        out_shape=jax.ShapeDtypeStruct((TOKENS, n_out), BF16),
        scratch_shapes=[pltpu.VMEM((D_MODEL, tn), BF16)],
        compiler_params=_params("arbitrary", "arbitrary"),
        name="in_proj_gates" if gates else "in_proj_mix",
    )(*args)


LOG2E = 1.4426950408889634


def _attn_kernel(sink_ref, q_ref, kv_ref, halo_ref, o_ref, kvext_ref, bias_ref, *, layer, tq):
    n_qb = tq // BLK
    i = pl.program_id(0)
    kvext_ref[0:BLK, :] = halo_ref[...]
    kvext_ref[BLK:, :] = kv_ref[...]

    @pl.when(i == 0)
    def _():
        row = lax.broadcasted_iota(jnp.int32, (BLK, 2 * BLK), 0)
        col = lax.broadcasted_iota(jnp.int32, (BLK, 2 * BLK), 1)
        dist = row + BLK - col
        in_window = (dist >= 0) & (dist < WINDOW)
        distf = dist.astype(F32)
        for h in range(N_HEADS):
            slope = 2.0 ** (-8.0 * (h + 1) / N_HEADS)
            t = jnp.where(in_window, (-slope * LOG2E) * distf, NEG_INF)
            bias_ref[0, h] = t
            bias_ref[1, h] = jnp.where(col >= BLK, t, NEG_INF)

    lane_kv = lax.broadcasted_iota(jnp.int32, (2 * BLK, LANES), 1) < HEAD_DIM
    lane_o = lax.broadcasted_iota(jnp.int32, (BLK, LANES), 1) < HEAD_DIM
    nt_dims = (((1,), (1,)), ((), ()))

    def body(b, carry):
        r0 = pl.multiple_of(b * BLK, BLK)
        first = ((i * n_qb + b) % (SEQ // BLK) == 0).astype(jnp.int32)

        band = kvext_ref[pl.ds(r0, 2 * BLK), :]
        kt = band[:, 0:LANES]
        vt = band[:, LANES:2 * LANES]
        kt_sw = pltpu.roll(kt, HEAD_DIM, 1)
        vt_sw = pltpu.roll(vt, HEAD_DIM, 1)
        zero = jnp.zeros_like(kt)
        one = jnp.ones_like(vt)
        qb = q_ref[pl.ds(r0, BLK), :]

        for kv in range(N_KV):
            k_here, k_other = (kt, kt_sw) if kv == 0 else (kt_sw, kt)
            v_here, v_other = (vt, vt_sw) if kv == 0 else (vt_sw, vt)
            k_rhs = (jnp.where(lane_kv, k_here, zero), jnp.where(lane_kv, zero, k_other))
            v_rhs = (jnp.where(lane_kv, v_here, one), jnp.where(lane_kv, one, v_other))
            base = kv * GQA * HEAD_DIM
            q4 = jnp.concatenate(
                [qb[:, base + p * LANES: base + (p + 1) * LANES] for p in range(GQA // 2)], axis=0)
            q4 = q4 * jnp.asarray(HEAD_DIM ** -0.5 * LOG2E, BF16)

            res = []
            mx = []
            for parity in range(2):
                s = lax.dot_general(q4, k_rhs[parity], nt_dims, preferred_element_type=F32)
                probs = []
                mrow = []
                for p in range(GQA // 2):
                    h = kv * GQA + 2 * p + parity
                    sp = s[p * BLK:(p + 1) * BLK] + bias_ref[first, h]
                    m = jnp.maximum(jnp.max(sp, axis=1, keepdims=True), sink_ref[layer, h] * LOG2E)
                    probs.append(jnp.exp2(sp - m).astype(BF16))
                    mrow.append(m)
                pmat = jnp.concatenate(probs, axis=0)
                res.append(jnp.dot(pmat, v_rhs[parity], preferred_element_type=F32))
                mx.append(mrow)

            for p in range(GQA // 2):
                h_e = kv * GQA + 2 * p
                r_e = res[0][p * BLK:(p + 1) * BLK]
                r_o = res[1][p * BLK:(p + 1) * BLK]
                numer = jnp.where(lane_o, r_e, r_o)
                sums = pltpu.roll(jnp.where(lane_o, r_o, r_e), HEAD_DIM, 1)
                sink_e = jnp.exp2(sink_ref[layer, h_e] * LOG2E - mx[0][p])
                sink_o = jnp.exp2(sink_ref[layer, h_e + 1] * LOG2E - mx[1][p])
                denom = sums + jnp.where(lane_o, sink_e, sink_o)
                o_ref[pl.ds(r0, BLK), base + p * LANES: base + (p + 1) * LANES] = (numer / denom).astype(o_ref.dtype)
        return carry

    lax.fori_loop(0, n_qb, body, 0, unroll=True)


def _attention(a, sinks, layer):
    tq = ATTN_TQ
    q_blk = COL_Q // ATTN_WIDTH
    kv_blk = COL_K // (2 * KV_WIDTH)
    return pl.pallas_call(
        functools.partial(_attn_kernel, layer=layer, tq=tq),
        grid=(TOKENS // tq,),
        in_specs=[pl.BlockSpec(memory_space=pltpu.SMEM),
                  pl.BlockSpec((tq, ATTN_WIDTH), lambda i: (i, q_blk)),
                  pl.BlockSpec((tq, 2 * KV_WIDTH), lambda i: (i, kv_blk)),
                  pl.BlockSpec((BLK, 2 * KV_WIDTH), lambda i: (jnp.maximum(i * (tq // BLK) - 1, 0), kv_blk))],
        out_specs=pl.BlockSpec((tq, ATTN_WIDTH), lambda i: (i, 0)),
        out_shape=jax.ShapeDtypeStruct((TOKENS, ATTN_WIDTH), BF16),
        scratch_shapes=[pltpu.VMEM((BLK + tq, 2 * KV_WIDTH), BF16),
                        pltpu.VMEM((2, N_HEADS, BLK, 2 * BLK), F32)],
        compiler_params=_params("arbitrary"),
        name="swa_attention",
    )(sinks, a, a, a)


POOL_HALO = 16
CONV_HALO = SUBLANES_F32
C_BLK = IN_PROJ_TN
C_CHUNK = 256


def _mix_kernel(u_ref, c0_ref, c1_ref, c2_ref, c3_ref, att_ref, g_ref, wgrp_ref, wpo_ref, wao_ref, wco_ref,
                scale_ref, wcm_ref, out_ref, uext_ref, zext_ref, pa_ref, pg_ref, cv_ref, *, tm):
    i = pl.program_id(0)
    seq_start = (i % (SEQ // tm)) == 0
    c_refs = (c0_ref, c1_ref, c2_ref, c3_ref)

    def conv_cols(col, width):
        blk, off = divmod(col, C_BLK)
        assert off + width <= C_BLK
        return c_refs[blk][:, off:off + width].astype(F32)

    @pl.when(seq_start)
    def _():
        uext_ref[0:POOL_HALO, :] = jnp.zeros((POOL_HALO, POOL_WIDTH), BF16)
        zext_ref[0:CONV_HALO, :] = jnp.zeros((CONV_HALO, CONV_WIDTH), F32)

    @pl.when(jnp.logical_not(seq_start))
    def _():
        uext_ref[0:POOL_HALO, :] = uext_ref[tm:tm + POOL_HALO, :]
        zext_ref[0:CONV_HALO, :] = zext_ref[tm:tm + CONV_HALO, :]

    uext_ref[POOL_HALO:, :] = u_ref[...]

    r_idx = lax.broadcasted_iota(jnp.int32, (BLK, BLK + POOL_HALO), 0)
    c_idx = lax.broadcasted_iota(jnp.int32, (BLK, BLK + POOL_HALO), 1)
    back = r_idx + POOL_HALO - c_idx
    bands = [jnp.where((back >= 0) & (back < w), 1.0, 0.0).astype(BF16) for w in POOL_WINDOWS]
    for rb in range(tm // BLK):
        pos = (i * tm + rb * BLK) % SEQ + lax.broadcasted_iota(jnp.int32, (BLK, 1), 0)
        for gi, w in enumerate(POOL_WINDOWS):
            cs = slice(gi * POOL_GROUP, (gi + 1) * POOL_GROUP)
            win = jnp.dot(bands[gi], uext_ref[rb * BLK:(rb + 1) * BLK + POOL_HALO, cs], preferred_element_type=F32)
            inv_cnt = 1.0 / jnp.minimum(pos + 1, w).astype(F32)
            tok = uext_ref[POOL_HALO + rb * BLK:POOL_HALO + (rb + 1) * BLK, cs].astype(F32)
            pa_ref[rb * BLK:(rb + 1) * BLK, cs] = (win * inv_cnt - tok).astype(BF16)
    for gi in range(len(POOL_WINDOWS)):
        cs = slice(gi * POOL_GROUP, (gi + 1) * POOL_GROUP)
        y = jnp.dot(pa_ref[:, cs], wgrp_ref[gi], preferred_element_type=F32) * scale_ref[:, cs]
        pg_ref[:, cs] = y.astype(BF16)

    for q in range(CONV_WIDTH // C_CHUNK):
        cs = slice(q * C_CHUNK, (q + 1) * C_CHUNK)
        zext_ref[CONV_HALO:, cs] = (conv_cols(CONV_WIDTH + q * C_CHUNK, C_CHUNK)
                                    * conv_cols(2 * CONV_WIDTH + q * C_CHUNK, C_CHUNK))
    for q in range(CONV_WIDTH // C_CHUNK):
        cs = slice(q * C_CHUNK, (q + 1) * C_CHUNK)
        y = (wcm_ref[0:1, cs] * zext_ref[CONV_HALO - 2:CONV_HALO - 2 + tm, cs]
             + wcm_ref[1:2, cs] * zext_ref[CONV_HALO - 1:CONV_HALO - 1 + tm, cs]
             + wcm_ref[2:3, cs] * zext_ref[CONV_HALO:CONV_HALO + tm, cs])
        cv_ref[:, cs] = (conv_cols(q * C_CHUNK, C_CHUNK) * y).astype(BF16)

    half = D_MODEL // 2
    for nc in range(2):
        cs = slice(nc * half, (nc + 1) * half)
        ya = jnp.dot(pg_ref[...], wpo_ref[:, cs], preferred_element_type=F32)
        yb = jnp.dot(att_ref[...], wao_ref[:, cs], preferred_element_type=F32)
        yc = jnp.dot(cv_ref[...], wco_ref[:, cs], preferred_element_type=F32)
        ga = g_ref[:, nc * half:(nc + 1) * half].astype(F32)
        gb = g_ref[:, D_MODEL + nc * half:D_MODEL + (nc + 1) * half].astype(F32)
        gc = g_ref[:, 2 * D_MODEL + nc * half:2 * D_MODEL + (nc + 1) * half].astype(F32)
        out_ref[:, cs] = (ga * ya + gb * yb + gc * yc).astype(out_ref.dtype)


def _mix(a, att, gates, wgrp, wpo, wao, wco, pool_scale3, w_conv_mix, layer):
    tm = MIX_TM
    c_blk0 = COL_CB // C_BLK
    assert COL_CB % C_BLK == 0 and (3 * CONV_WIDTH) % C_BLK == 0
    c_specs = [pl.BlockSpec((tm, C_BLK), functools.partial(lambda i, k: (i, c_blk0 + k), k=k)) for k in range(4)]
    return pl.pallas_call(
        functools.partial(_mix_kernel, tm=tm),
        grid=(TOKENS // tm,),
        in_specs=[pl.BlockSpec((tm, POOL_WIDTH), lambda i: (i, 0))] + c_specs + [
            pl.BlockSpec((tm, ATTN_WIDTH), lambda i: (i, 0)),
            pl.BlockSpec((tm, N_GATE), lambda i: (i, 0)),
            _resident((None, len(POOL_WINDOWS), POOL_GROUP, POOL_GROUP), lambda i: (layer, 0, 0, 0)),
            _resident((None, POOL_WIDTH, D_MODEL), lambda i: (layer, 0, 0)),
            _resident((None, ATTN_WIDTH, D_MODEL), lambda i: (layer, 0, 0)),
            _resident((None, CONV_WIDTH, D_MODEL), lambda i: (layer, 0, 0)),
            _resident((None, 1, POOL_WIDTH), lambda i: (layer, 0, 0)),
            _resident((None, 3, CONV_WIDTH), lambda i: (layer, 0, 0)),
        ],
        out_specs=pl.BlockSpec((tm, D_MODEL), lambda i: (i, 0)),
        out_shape=jax.ShapeDtypeStruct((TOKENS, D_MODEL), BF16),
        scratch_shapes=[pltpu.VMEM((POOL_HALO + tm, POOL_WIDTH), BF16),
                        pltpu.VMEM((CONV_HALO + tm, CONV_WIDTH), F32),
                        pltpu.VMEM((tm, POOL_WIDTH), BF16),
                        pltpu.VMEM((tm, POOL_WIDTH), BF16),
                        pltpu.VMEM((tm, CONV_WIDTH), BF16)],
        compiler_params=_params("arbitrary"),
        name="mixers_merge",
    )(a, a, a, a, a, att, gates, wgrp, wpo, wao, wco, pool_scale3, w_conv_mix)


def _proj_residual_kernel(a_ref, x_ref, w_ref, g_ref, *out_refs, final):
    x_new = x_ref[...] + jnp.dot(a_ref[...], w_ref[...], preferred_element_type=F32)
    normed = _rms_scale(x_new, g_ref[...])
    if final:
        out_refs[0][...] = normed
    else:
        out_refs[0][...] = x_new
        out_refs[1][...] = normed.astype(out_refs[1].dtype)


def _proj_residual(act, x, w, g3, w_layer, g_layer, tm, final, name):
    k = act.shape[1]
    tok_spec = pl.BlockSpec((tm, D_MODEL), lambda i: (i, 0))
    if final:
        out_specs = tok_spec
        out_shape = jax.ShapeDtypeStruct((TOKENS, D_MODEL), F32)
    else:
        out_specs = [tok_spec, tok_spec]
        out_shape = [jax.ShapeDtypeStruct((TOKENS, D_MODEL), F32), jax.ShapeDtypeStruct((TOKENS, D_MODEL), BF16)]
    g_spec = (_resident((1, D_MODEL), lambda i: (0, 0)) if g_layer is None
              else _resident((None, 1, D_MODEL), lambda i: (g_layer, 0, 0)))
    return pl.pallas_call(
        functools.partial(_proj_residual_kernel, final=final),
        grid=(TOKENS // tm,),
        in_specs=[pl.BlockSpec((tm, k), lambda i: (i, 0)),
                  tok_spec,
                  _resident((None, k, D_MODEL), lambda i: (w_layer, 0, 0)),
                  g_spec],
        out_specs=out_specs,
        out_shape=out_shape,
        compiler_params=_params("arbitrary"),
        name=name,
    )(act, x, w, g3)


def _up_kernel(h_ref, wg_ref, wv_ref, cwg_ref, cwv_ref, bg_ref, bv_ref, o_ref, wbf_ref, zext_ref, *, tm, tn):
    m = pl.program_id(1)

    @pl.when(m == 0)
    def _():
        wbf_ref[:, 0:tn] = wg_ref[...].astype(BF16)
        wbf_ref[:, tn:2 * tn] = wv_ref[...].astype(BF16)

    seq_start = (m % (SEQ // tm)) == 0

    @pl.when(seq_start)
    def _():
        zext_ref[0:CONV_HALO, :] = jnp.zeros((CONV_HALO, 2 * tn), F32)

    @pl.when(jnp.logical_not(seq_start))
    def _():
        zext_ref[0:CONV_HALO, :] = zext_ref[tm:tm + CONV_HALO, :]

    zext_ref[CONV_HALO:, :] = jnp.dot(h_ref[...], wbf_ref[...], preferred_element_type=F32)

    def conv(cs, cw_ref, b_ref, scale):
        taps = cw_ref[...] * scale
        return (taps[0:1] * zext_ref[CONV_HALO - 2:CONV_HALO - 2 + tm, cs]
                + taps[1:2] * zext_ref[CONV_HALO - 1:CONV_HALO - 1 + tm, cs]
                + taps[2:3] * zext_ref[CONV_HALO:CONV_HALO + tm, cs]
                + b_ref[...] * scale)

    half_gate = conv(slice(0, tn), cwg_ref, bg_ref, 0.5)
    u_val = conv(slice(tn, 2 * tn), cwv_ref, bv_ref, 1.0)
    o_ref[...] = (half_gate * (1.0 + jnp.tanh(half_gate)) * u_val).astype(o_ref.dtype)


def _up(h2, w_up, w_ffn_conv, b_ffn_conv3, layer):
    tm, tn = UP_TM, UP_TN
    nj = D_FF // tn
    w_spec = lambda off: pl.BlockSpec((None, D_MODEL, tn), lambda j, m: (layer, 0, j + off),
                                      pipeline_mode=pl.Buffered(1))
    return pl.pallas_call(
        functools.partial(_up_kernel, tm=tm, tn=tn),
        grid=(nj, TOKENS // tm),
        in_specs=[pl.BlockSpec((tm, D_MODEL), lambda j, m: (m, 0)),
                  w_spec(0),
                  w_spec(nj),
                  pl.BlockSpec((None, 3, tn), lambda j, m: (layer, 0, j)),
                  pl.BlockSpec((None, 3, tn), lambda j, m: (layer, 0, j + nj)),
                  pl.BlockSpec((None, 1, tn), lambda j, m: (layer, 0, j)),
                  pl.BlockSpec((None, 1, tn), lambda j, m: (layer, 0, j + nj))],
        out_specs=pl.BlockSpec((tm, tn), lambda j, m: (m, j)),
        out_shape=jax.ShapeDtypeStruct((TOKENS, D_FF), BF16),
        scratch_shapes=[pltpu.VMEM((D_MODEL, 2 * tn), BF16),
                        pltpu.VMEM((CONV_HALO + tm, 2 * tn), F32)],
        compiler_params=_params("arbitrary", "arbitrary"),
        name="up_conv_act",
    )(h2, w_up, w_up, w_ffn_conv, w_ffn_conv, b_ffn_conv3, b_ffn_conv3)


def kernel(x, g_mix, w_in, b_gate, w_pool_grp, pool_scale, w_pool_out, sinks, w_attn_out, w_conv_mix, w_conv_out,
           w_o, g_ffn, w_up, w_ffn_conv, b_ffn_conv, w_down, g_final):
    assert x.shape == (BATCH, SEQ, D_MODEL) and x.dtype == F32
    xt = x.reshape(TOKENS, D_MODEL)
    g_mix3 = g_mix.reshape(DEPTH, 1, D_MODEL)
    g_ffn3 = g_ffn.reshape(DEPTH, 1, D_MODEL)
    g_final2 = g_final.reshape(1, D_MODEL)
    b_gate3 = b_gate.reshape(DEPTH, 1, N_GATE)
    pool_scale3 = pool_scale.reshape(DEPTH, 1, POOL_WIDTH)
    b_ffn3 = b_ffn_conv.reshape(DEPTH, 1, 2 * D_FF)
    wgrp, wpo, wao, wco, wo, wdn = (w.astype(BF16) for w in (w_pool_grp, w_pool_out, w_attn_out, w_conv_out, w_o, w_down))

    h = _rmsnorm_bf16(xt, g_mix3, 0)
    for layer in range(DEPTH):
        a = _in_proj(h, w_in, None, layer, gates=False)
        gates = _in_proj(h, w_in, b_gate3, layer, gates=True)
        att = _attention(a, sinks, layer)
        merged = _mix(a, att, gates, wgrp, wpo, wao, wco, pool_scale3, w_conv_mix, layer)
        xt, h2 = _proj_residual(merged, xt, wo, g_ffn3, layer, layer, OPROJ_TM, False, "out_proj_norm")
        act = _up(h2, w_up, w_ffn_conv, b_ffn3, layer)
        if layer + 1 < DEPTH:
            xt, h = _proj_residual(act, xt, wdn, g_mix3, layer, layer + 1, DOWN_TM, False, "down_proj_norm")
        else:
            out = _proj_residual(act, xt, wdn, g_final2, layer, None, DOWN_TM, True, "down_proj_final")
    return out.reshape(BATCH, SEQ, D_MODEL)
```

```python
import functools

import jax
import jax.numpy as jnp
from jax import lax
from jax.experimental import pallas as pl
from jax.experimental.pallas import tpu as pltpu

D_MODEL = 2048
BATCH = 2
SEQ = 16384
DEPTH = 2
TOKENS = BATCH * SEQ

POOL_WIDTH = D_MODEL // 2
POOL_WINDOWS = (2, 4, 8, 16)
POOL_GROUP = POOL_WIDTH // len(POOL_WINDOWS)
HEAD_DIM = 64
N_HEADS = D_MODEL // 128
GQA = 8
N_KV = N_HEADS // GQA
WINDOW = 128
BLK = 128
CONV_WIDTH = D_MODEL // 2
D_FF = ((8 * D_MODEL // 3 + 255) // 256) * 256
N_BRANCH = 3
RMS_EPS = 1e-6
NEG_INF = -1e30

ATTN_WIDTH = N_HEADS * HEAD_DIM
KV_WIDTH = N_KV * HEAD_DIM
COL_U = 0
COL_Q = COL_U + POOL_WIDTH
COL_K = COL_Q + ATTN_WIDTH
COL_V = COL_K + KV_WIDTH
COL_CB = COL_V + KV_WIDTH
COL_CC = COL_CB + CONV_WIDTH
COL_CH = COL_CC + CONV_WIDTH
COL_GATE = COL_CH + CONV_WIDTH
N_MIX_IN = COL_GATE
N_GATE = N_BRANCH * D_MODEL

LANES = 128
SUBLANES_F32 = 8
SUBLANES_BF16 = 16
VMEM_LIMIT_BYTES = 56 * 1024 * 1024

BF16 = jnp.bfloat16
F32 = jnp.float32

IN_PROJ_TN = 768
IN_PROJ_TM = 2048
ATTN_TQ = 512
MIX_TM = 512
OPROJ_TM = 512
UP_TM = 512
UP_TN = 1408
DOWN_TM = 256
NORM_TM = 1024


def _params(*sem):
    return pltpu.CompilerParams(dimension_semantics=sem, vmem_limit_bytes=VMEM_LIMIT_BYTES)


def _resident(shape, index_map):
    return pl.BlockSpec(shape, index_map, pipeline_mode=pl.Buffered(1))


def _rms_scale(x, g):
    ms = jnp.mean(x * x, axis=-1, keepdims=True)
    return x * lax.rsqrt(ms + RMS_EPS) * g


def _rmsnorm_kernel(x_ref, g_ref, o_ref):
    o_ref[...] = _rms_scale(x_ref[...], g_ref[...]).astype(o_ref.dtype)


def _rmsnorm_bf16(x, g3, layer):
    tm = NORM_TM
    return pl.pallas_call(
        _rmsnorm_kernel,
        grid=(TOKENS // tm,),
        in_specs=[pl.BlockSpec((tm, D_MODEL), lambda i: (i, 0)),
                  pl.BlockSpec((None, 1, D_MODEL), lambda i: (layer, 0, 0))],
        out_specs=pl.BlockSpec((tm, D_MODEL), lambda i: (i, 0)),
        out_shape=jax.ShapeDtypeStruct((TOKENS, D_MODEL), BF16),
        compiler_params=_params("arbitrary"),
        name="rmsnorm_in",
    )(x, g3)


def _in_proj_kernel(h_ref, w_ref, o_ref, wbf_ref):
    @pl.when(pl.program_id(1) == 0)
    def _():
        wbf_ref[...] = w_ref[...].astype(BF16)

    acc = jnp.dot(h_ref[...], wbf_ref[...], preferred_element_type=F32)
    o_ref[...] = acc.astype(o_ref.dtype)


def _in_proj_gate_kernel(h_ref, w_ref, b_ref, o_ref, wbf_ref):
    @pl.when(pl.program_id(1) == 0)
    def _():
        wbf_ref[...] = (0.5 * w_ref[...]).astype(BF16)

    half_z = jnp.dot(h_ref[...], wbf_ref[...], preferred_element_type=F32) + 0.5 * b_ref[...]
    o_ref[...] = (0.5 * jnp.tanh(half_z) + 0.5).astype(o_ref.dtype)


def _in_proj(h, w_in, b_gate3, layer, gates):
    tm, tn = IN_PROJ_TM, IN_PROJ_TN
    n_out = N_GATE if gates else N_MIX_IN
    col0 = (COL_GATE if gates else 0) // tn
    in_specs = [pl.BlockSpec((tm, D_MODEL), lambda j, m: (m, 0)),
                pl.BlockSpec((None, D_MODEL, tn), lambda j, m: (layer, 0, j + col0))]
    args = [h, w_in]
    if gates:
        in_specs.append(pl.BlockSpec((None, 1, tn), lambda j, m: (layer, 0, j)))
        args.append(b_gate3)
    return pl.pallas_call(
        _in_proj_gate_kernel if gates else _in_proj_kernel,
        grid=(n_out // tn, TOKENS // tm),
        in_specs=in_specs,
        out_specs=pl.BlockSpec((tm, tn), lambda j, m: (m, j)),
        out_shape=jax.ShapeDtypeStruct((TOKENS, n_out), BF16),
        scratch_shapes=[pltpu.VMEM((D_MODEL, tn), BF16)],
        compiler_params=_params("arbitrary", "arbitrary"),
        name="in_proj_gates" if gates else "in_proj_mix",
    )(*args)


LOG2E = 1.4426950408889634


def _attn_kernel(sink_ref, q_ref, kv_ref, halo_ref, o_ref, kvext_ref, bias_ref, *, layer, tq):
    n_qb = tq // BLK
    i = pl.program_id(0)
    kvext_ref[0:BLK, :] = halo_ref[...]
    kvext_ref[BLK:, :] = kv_ref[...]

    @pl.when(i == 0)
    def _():
        row = lax.broadcasted_iota(jnp.int32, (BLK, 2 * BLK), 0)
        col = lax.broadcasted_iota(jnp.int32, (BLK, 2 * BLK), 1)
        dist = row + BLK - col
        in_window = (dist >= 0) & (dist < WINDOW)
        distf = dist.astype(F32)
        for h in range(N_HEADS):
            slope = 2.0 ** (-8.0 * (h + 1) / N_HEADS)
            t = jnp.where(in_window, (-slope * LOG2E) * distf, NEG_INF)
            bias_ref[0, h] = t
            bias_ref[1, h] = jnp.where(col >= BLK, t, NEG_INF)

    lane_kv = lax.broadcasted_iota(jnp.int32, (2 * BLK, LANES), 1) < HEAD_DIM
    lane_o = lax.broadcasted_iota(jnp.int32, (BLK, LANES), 1) < HEAD_DIM
    nt_dims = (((1,), (1,)), ((), ()))

    def body(b, carry):
        r0 = pl.multiple_of(b * BLK, BLK)
        first = ((i * n_qb + b) % (SEQ // BLK) == 0).astype(jnp.int32)

        band = kvext_ref[pl.ds(r0, 2 * BLK), :]
        kt = band[:, 0:LANES]
        vt = band[:, LANES:2 * LANES]
        kt_sw = pltpu.roll(kt, HEAD_DIM, 1)
        vt_sw = pltpu.roll(vt, HEAD_DIM, 1)
        zero = jnp.zeros_like(kt)
        one = jnp.ones_like(vt)
        qb = q_ref[pl.ds(r0, BLK), :]

        for kv in range(N_KV):
            k_here, k_other = (kt, kt_sw) if kv == 0 else (kt_sw, kt)
            v_here, v_other = (vt, vt_sw) if kv == 0 else (vt_sw, vt)
            k_rhs = (jnp.where(lane_kv, k_here, zero), jnp.where(lane_kv, zero, k_other))
            v_rhs = (jnp.where(lane_kv, v_here, one), jnp.where(lane_kv, one, v_other))
            base = kv * GQA * HEAD_DIM
            q4 = jnp.concatenate(
                [qb[:, base + p * LANES: base + (p + 1) * LANES] for p in range(GQA // 2)], axis=0)
            q4 = q4 * jnp.asarray(HEAD_DIM ** -0.5 * LOG2E, BF16)

            res = []
            mx = []
            for parity in range(2):
                s = lax.dot_general(q4, k_rhs[parity], nt_dims, preferred_element_type=F32)
                probs = []
                mrow = []
                for p in range(GQA // 2):
                    h = kv * GQA + 2 * p + parity
                    sp = s[p * BLK:(p + 1) * BLK] + bias_ref[first, h]
                    m = jnp.maximum(jnp.max(sp, axis=1, keepdims=True), sink_ref[layer, h] * LOG2E)
                    probs.append(jnp.exp2(sp - m).astype(BF16))
                    mrow.append(m)
                pmat = jnp.concatenate(probs, axis=0)
                res.append(jnp.dot(pmat, v_rhs[parity], preferred_element_type=F32))
                mx.append(mrow)

            for p in range(GQA // 2):
                h_e = kv * GQA + 2 * p
                r_e = res[0][p * BLK:(p + 1) * BLK]
                r_o = res[1][p * BLK:(p + 1) * BLK]
                numer = jnp.where(lane_o, r_e, r_o)
                sums = pltpu.roll(jnp.where(lane_o, r_o, r_e), HEAD_DIM, 1)
                sink_e = jnp.exp2(sink_ref[layer, h_e] * LOG2E - mx[0][p])
                sink_o = jnp.exp2(sink_ref[layer, h_e + 1] * LOG2E - mx[1][p])
                denom = sums + jnp.where(lane_o, sink_e, sink_o)
                o_ref[pl.ds(r0, BLK), base + p * LANES: base + (p + 1) * LANES] = (numer / denom).astype(o_ref.dtype)
        return carry

    lax.fori_loop(0, n_qb, body, 0, unroll=True)


def _attention(a, sinks, layer):
    tq = ATTN_TQ
    q_blk = COL_Q // ATTN_WIDTH
    kv_blk = COL_K // (2 * KV_WIDTH)
    return pl.pallas_call(
        functools.partial(_attn_kernel, layer=layer, tq=tq),
        grid=(TOKENS // tq,),
        in_specs=[pl.BlockSpec(memory_space=pltpu.SMEM),
                  pl.BlockSpec((tq, ATTN_WIDTH), lambda i: (i, q_blk)),
                  pl.BlockSpec((tq, 2 * KV_WIDTH), lambda i: (i, kv_blk)),
                  pl.BlockSpec((BLK, 2 * KV_WIDTH), lambda i: (jnp.maximum(i * (tq // BLK) - 1, 0), kv_blk))],
        out_specs=pl.BlockSpec((tq, ATTN_WIDTH), lambda i: (i, 0)),
        out_shape=jax.ShapeDtypeStruct((TOKENS, ATTN_WIDTH), BF16),
        scratch_shapes=[pltpu.VMEM((BLK + tq, 2 * KV_WIDTH), BF16),
                        pltpu.VMEM((2, N_HEADS, BLK, 2 * BLK), F32)],
        compiler_params=_params("arbitrary"),
        name="swa_attention",
    )(sinks, a, a, a)


POOL_HALO = 16
CONV_HALO = SUBLANES_F32
C_BLK = IN_PROJ_TN
C_CHUNK = 256


def _mix_kernel(u_ref, c0_ref, c1_ref, c2_ref, c3_ref, att_ref, g_ref, wgrp_ref, wpo_ref, wao_ref, wco_ref,
                scale_ref, wcm_ref, out_ref, uext_ref, zext_ref, pa_ref, pg_ref, cv_ref, *, tm):
    i = pl.program_id(0)
    seq_start = (i % (SEQ // tm)) == 0
    c_refs = (c0_ref, c1_ref, c2_ref, c3_ref)

    def conv_cols(col, width):
        blk, off = divmod(col, C_BLK)
        assert off + width <= C_BLK
        return c_refs[blk][:, off:off + width].astype(F32)

    @pl.when(seq_start)
    def _():
        uext_ref[0:POOL_HALO, :] = jnp.zeros((POOL_HALO, POOL_WIDTH), BF16)
        zext_ref[0:CONV_HALO, :] = jnp.zeros((CONV_HALO, CONV_WIDTH), F32)

    @pl.when(jnp.logical_not(seq_start))
    def _():
        uext_ref[0:POOL_HALO, :] = uext_ref[tm:tm + POOL_HALO, :]
        zext_ref[0:CONV_HALO, :] = zext_ref[tm:tm + CONV_HALO, :]

    uext_ref[POOL_HALO:, :] = u_ref[...]

    r_idx = lax.broadcasted_iota(jnp.int32, (BLK, BLK + POOL_HALO), 0)
    c_idx = lax.broadcasted_iota(jnp.int32, (BLK, BLK + POOL_HALO), 1)
    back = r_idx + POOL_HALO - c_idx
    bands = [jnp.where((back >= 0) & (back < w), 1.0, 0.0).astype(BF16) for w in POOL_WINDOWS]
    for rb in range(tm // BLK):
        pos = (i * tm + rb * BLK) % SEQ + lax.broadcasted_iota(jnp.int32, (BLK, 1), 0)
        for gi, w in enumerate(POOL_WINDOWS):
            cs = slice(gi * POOL_GROUP, (gi + 1) * POOL_GROUP)
            win = jnp.dot(bands[gi], uext_ref[rb * BLK:(rb + 1) * BLK + POOL_HALO, cs], preferred_element_type=F32)
            inv_cnt = 1.0 / jnp.minimum(pos + 1, w).astype(F32)
            tok = uext_ref[POOL_HALO + rb * BLK:POOL_HALO + (rb + 1) * BLK, cs].astype(F32)
            pa_ref[rb * BLK:(rb + 1) * BLK, cs] = (win * inv_cnt - tok).astype(BF16)
    for gi in range(len(POOL_WINDOWS)):
        cs = slice(gi * POOL_GROUP, (gi + 1) * POOL_GROUP)
        y = jnp.dot(pa_ref[:, cs], wgrp_ref[gi], preferred_element_type=F32) * scale_ref[:, cs]
        pg_ref[:, cs] = y.astype(BF16)

    for q in range(CONV_WIDTH // C_CHUNK):
        cs = slice(q * C_CHUNK, (q + 1) * C_CHUNK)
        zext_ref[CONV_HALO:, cs] = (conv_cols(CONV_WIDTH + q * C_CHUNK, C_CHUNK)
                                    * conv_cols(2 * CONV_WIDTH + q * C_CHUNK, C_CHUNK))
    for q in range(CONV_WIDTH // C_CHUNK):
        cs = slice(q * C_CHUNK, (q + 1) * C_CHUNK)
        ext = zext_ref[:, cs]
        y = (wcm_ref[0:1, cs] * pltpu.roll(ext, 2, 0)[CONV_HALO:]
             + wcm_ref[1:2, cs] * pltpu.roll(ext, 1, 0)[CONV_HALO:]
             + wcm_ref[2:3, cs] * ext[CONV_HALO:])
        cv_ref[:, cs] = (conv_cols(q * C_CHUNK, C_CHUNK) * y).astype(BF16)

    half = D_MODEL // 2
    for nc in range(2):
        cs = slice(nc * half, (nc + 1) * half)
        ya = jnp.dot(pg_ref[...], wpo_ref[:, cs], preferred_element_type=F32)
        yb = jnp.dot(att_ref[...], wao_ref[:, cs], preferred_element_type=F32)
        yc = jnp.dot(cv_ref[...], wco_ref[:, cs], preferred_element_type=F32)
        ga = g_ref[:, nc * half:(nc + 1) * half].astype(F32)
        gb = g_ref[:, D_MODEL + nc * half:D_MODEL + (nc + 1) * half].astype(F32)
        gc = g_ref[:, 2 * D_MODEL + nc * half:2 * D_MODEL + (nc + 1) * half].astype(F32)
        out_ref[:, cs] = (ga * ya + gb * yb + gc * yc).astype(out_ref.dtype)


def _mix(a, att, gates, wgrp, wpo, wao, wco, pool_scale3, w_conv_mix, layer):
    tm = MIX_TM
    c_blk0 = COL_CB // C_BLK
    assert COL_CB % C_BLK == 0 and (3 * CONV_WIDTH) % C_BLK == 0
    c_specs = [pl.BlockSpec((tm, C_BLK), functools.partial(lambda i, k: (i, c_blk0 + k), k=k)) for k in range(4)]
    return pl.pallas_call(
        functools.partial(_mix_kernel, tm=tm),
        grid=(TOKENS // tm,),
        in_specs=[pl.BlockSpec((tm, POOL_WIDTH), lambda i: (i, 0))] + c_specs + [
            pl.BlockSpec((tm, ATTN_WIDTH), lambda i: (i, 0)),
            pl.BlockSpec((tm, N_GATE), lambda i: (i, 0)),
            _resident((None, len(POOL_WINDOWS), POOL_GROUP, POOL_GROUP), lambda i: (layer, 0, 0, 0)),
            _resident((None, POOL_WIDTH, D_MODEL), lambda i: (layer, 0, 0)),
            _resident((None, ATTN_WIDTH, D_MODEL), lambda i: (layer, 0, 0)),
            _resident((None, CONV_WIDTH, D_MODEL), lambda i: (layer, 0, 0)),
            _resident((None, 1, POOL_WIDTH), lambda i: (layer, 0, 0)),
            _resident((None, 3, CONV_WIDTH), lambda i: (layer, 0, 0)),
        ],
        out_specs=pl.BlockSpec((tm, D_MODEL), lambda i: (i, 0)),
        out_shape=jax.ShapeDtypeStruct((TOKENS, D_MODEL), BF16),
        scratch_shapes=[pltpu.VMEM((POOL_HALO + tm, POOL_WIDTH), BF16),
                        pltpu.VMEM((CONV_HALO + tm, CONV_WIDTH), F32),
                        pltpu.VMEM((tm, POOL_WIDTH), BF16),
                        pltpu.VMEM((tm, POOL_WIDTH), BF16),
                        pltpu.VMEM((tm, CONV_WIDTH), BF16)],
        compiler_params=_params("arbitrary"),
        name="mixers_merge",
    )(a, a, a, a, a, att, gates, wgrp, wpo, wao, wco, pool_scale3, w_conv_mix)


def _proj_residual_kernel(a_ref, x_ref, w_ref, g_ref, *out_refs, final):
    x_new = x_ref[...] + jnp.dot(a_ref[...], w_ref[...], preferred_element_type=F32)
    normed = _rms_scale(x_new, g_ref[...])
    if final:
        out_refs[0][...] = normed
    else:
        out_refs[0][...] = x_new
        out_refs[1][...] = normed.astype(out_refs[1].dtype)


def _proj_residual(act, x, w, g3, w_layer, g_layer, tm, final, name):
    k = act.shape[1]
    tok_spec = pl.BlockSpec((tm, D_MODEL), lambda i: (i, 0))
    if final:
        out_specs = tok_spec
        out_shape = jax.ShapeDtypeStruct((TOKENS, D_MODEL), F32)
    else:
        out_specs = [tok_spec, tok_spec]
        out_shape = [jax.ShapeDtypeStruct((TOKENS, D_MODEL), F32), jax.ShapeDtypeStruct((TOKENS, D_MODEL), BF16)]
    g_spec = (_resident((1, D_MODEL), lambda i: (0, 0)) if g_layer is None
              else _resident((None, 1, D_MODEL), lambda i: (g_layer, 0, 0)))
    return pl.pallas_call(
        functools.partial(_proj_residual_kernel, final=final),
        grid=(TOKENS // tm,),
        in_specs=[pl.BlockSpec((tm, k), lambda i: (i, 0)),
                  tok_spec,
                  _resident((None, k, D_MODEL), lambda i: (w_layer, 0, 0)),
                  g_spec],
        out_specs=out_specs,
        out_shape=out_shape,
        compiler_params=_params("arbitrary"),
        name=name,
    )(act, x, w, g3)


def _up_kernel(h_ref, wg_ref, wv_ref, cwg_ref, cwv_ref, bg_ref, bv_ref, o_ref, wbf_ref, zext_ref, *, tm, tn):
    m = pl.program_id(1)

    @pl.when(m == 0)
    def _():
        wbf_ref[:, 0:tn] = wg_ref[...].astype(BF16)
        wbf_ref[:, tn:2 * tn] = wv_ref[...].astype(BF16)

    seq_start = (m % (SEQ // tm)) == 0

    @pl.when(seq_start)
    def _():
        zext_ref[0:CONV_HALO, :] = jnp.zeros((CONV_HALO, 2 * tn), F32)

    @pl.when(jnp.logical_not(seq_start))
    def _():
        zext_ref[0:CONV_HALO, :] = zext_ref[tm:tm + CONV_HALO, :]

    zext_ref[CONV_HALO:, :] = jnp.dot(h_ref[...], wbf_ref[...], preferred_element_type=F32)

    def conv(cs, cw_ref, b_ref, scale):
        taps = cw_ref[...] * scale
        ext = zext_ref[:, cs]
        back1 = pltpu.roll(ext, 1, 0)[CONV_HALO:]
        back2 = pltpu.roll(ext, 2, 0)[CONV_HALO:]
        return (taps[0:1] * back2 + taps[1:2] * back1 + taps[2:3] * ext[CONV_HALO:] + b_ref[...] * scale)

    half_gate = conv(slice(0, tn), cwg_ref, bg_ref, 0.5)
    u_val = conv(slice(tn, 2 * tn), cwv_ref, bv_ref, 1.0)
    o_ref[...] = (half_gate * (1.0 + jnp.tanh(half_gate)) * u_val).astype(o_ref.dtype)


def _up(h2, w_up, w_ffn_conv, b_ffn_conv3, layer):
    tm, tn = UP_TM, UP_TN
    nj = D_FF // tn
    w_spec = lambda off: pl.BlockSpec((None, D_MODEL, tn), lambda j, m: (layer, 0, j + off),
                                      pipeline_mode=pl.Buffered(1))
    return pl.pallas_call(
        functools.partial(_up_kernel, tm=tm, tn=tn),
        grid=(nj, TOKENS // tm),
        in_specs=[pl.BlockSpec((tm, D_MODEL), lambda j, m: (m, 0)),
                  w_spec(0),
                  w_spec(nj),
                  pl.BlockSpec((None, 3, tn), lambda j, m: (layer, 0, j)),
                  pl.BlockSpec((None, 3, tn), lambda j, m: (layer, 0, j + nj)),
                  pl.BlockSpec((None, 1, tn), lambda j, m: (layer, 0, j)),
                  pl.BlockSpec((None, 1, tn), lambda j, m: (layer, 0, j + nj))],
        out_specs=pl.BlockSpec((tm, tn), lambda j, m: (m, j)),
        out_shape=jax.ShapeDtypeStruct((TOKENS, D_FF), BF16),
        scratch_shapes=[pltpu.VMEM((D_MODEL, 2 * tn), BF16),
                        pltpu.VMEM((CONV_HALO + tm, 2 * tn), F32)],
        compiler_params=_params("arbitrary", "arbitrary"),
        name="up_conv_act",
    )(h2, w_up, w_up, w_ffn_conv, w_ffn_conv, b_ffn_conv3, b_ffn_conv3)


def kernel(x, g_mix, w_in, b_gate, w_pool_grp, pool_scale, w_pool_out, sinks, w_attn_out, w_conv_mix, w_conv_out,
           w_o, g_ffn, w_up, w_ffn_conv, b_ffn_conv, w_down, g_final):
    assert x.shape == (BATCH, SEQ, D_MODEL) and x.dtype == F32
    xt = x.reshape(TOKENS, D_MODEL)
    g_mix3 = g_mix.reshape(DEPTH, 1, D_MODEL)
    g_ffn3 = g_ffn.reshape(DEPTH, 1, D_MODEL)
    g_final2 = g_final.reshape(1, D_MODEL)
    b_gate3 = b_gate.reshape(DEPTH, 1, N_GATE)
    pool_scale3 = pool_scale.reshape(DEPTH, 1, POOL_WIDTH)
    b_ffn3 = b_ffn_conv.reshape(DEPTH, 1, 2 * D_FF)
    wgrp, wpo, wao, wco, wo, wdn = (w.astype(BF16) for w in (w_pool_grp, w_pool_out, w_attn_out, w_conv_out, w_o, w_down))

    h = _rmsnorm_bf16(xt, g_mix3, 0)
    for layer in range(DEPTH):
        a = _in_proj(h, w_in, None, layer, gates=False)
        gates = _in_proj(h, w_in, b_gate3, layer, gates=True)
        att = _attention(a, sinks, layer)
        merged = _mix(a, att, gates, wgrp, wpo, wao, wco, pool_scale3, w_conv_mix, layer)
        xt, h2 = _proj_residual(merged, xt, wo, g_ffn3, layer, layer, OPROJ_TM, False, "out_proj_norm")
        act = _up(h2, w_up, w_ffn_conv, b_ffn3, layer)
        if layer + 1 < DEPTH:
            xt, h = _proj_residual(act, xt, wdn, g_mix3, layer, layer + 1, DOWN_TM, False, "down_proj_norm")
        else:
            out = _proj_residual(act, xt, wdn, g_final2, layer, None, DOWN_TM, True, "down_proj_final")
    return out.reshape(BATCH, SEQ, D_MODEL)
```

```python
import functools

import jax
import jax.numpy as jnp
from jax import lax
from jax.experimental import pallas as pl
from jax.experimental.pallas import tpu as pltpu

D_MODEL = 2048
BATCH = 2
SEQ = 16384
DEPTH = 2
TOKENS = BATCH * SEQ

POOL_WIDTH = D_MODEL // 2
POOL_WINDOWS = (2, 4, 8, 16)
POOL_GROUP = POOL_WIDTH // len(POOL_WINDOWS)
HEAD_DIM = 64
N_HEADS = D_MODEL // 128
GQA = 8
N_KV = N_HEADS // GQA
WINDOW = 128
BLK = 128
CONV_WIDTH = D_MODEL // 2
D_FF = ((8 * D_MODEL // 3 + 255) // 256) * 256
N_BRANCH = 3
RMS_EPS = 1e-6
NEG_INF = -1e30

ATTN_WIDTH = N_HEADS * HEAD_DIM
KV_WIDTH = N_KV * HEAD_DIM
COL_U = 0
COL_Q = COL_U + POOL_WIDTH
COL_K = COL_Q + ATTN_WIDTH
COL_V = COL_K + KV_WIDTH
COL_CB = COL_V + KV_WIDTH
COL_CC = COL_CB + CONV_WIDTH
COL_CH = COL_CC + CONV_WIDTH
COL_GATE = COL_CH + CONV_WIDTH
N_MIX_IN = COL_GATE
N_GATE = N_BRANCH * D_MODEL

LANES = 128
SUBLANES_F32 = 8
SUBLANES_BF16 = 16
VMEM_LIMIT_BYTES = 56 * 1024 * 1024

BF16 = jnp.bfloat16
F32 = jnp.float32

IN_PROJ_TN = 768
IN_PROJ_TM = 2048
ATTN_TQ = 1024
MIX_TM = 512
OPROJ_TM = 512
UP_TM = 512
UP_TN = 1408
DOWN_TM = 256
NORM_TM = 1024


def _params(*sem):
    return pltpu.CompilerParams(dimension_semantics=sem, vmem_limit_bytes=VMEM_LIMIT_BYTES)


def _resident(shape, index_map):
    return pl.BlockSpec(shape, index_map, pipeline_mode=pl.Buffered(1))


def _rms_scale(x, g):
    ms = jnp.mean(x * x, axis=-1, keepdims=True)
    return x * lax.rsqrt(ms + RMS_EPS) * g


def _rmsnorm_kernel(x_ref, g_ref, o_ref):
    o_ref[...] = _rms_scale(x_ref[...], g_ref[...]).astype(o_ref.dtype)


def _rmsnorm_bf16(x, g3, layer):
    tm = NORM_TM
    return pl.pallas_call(
        _rmsnorm_kernel,
        grid=(TOKENS // tm,),
        in_specs=[pl.BlockSpec((tm, D_MODEL), lambda i: (i, 0)),
                  pl.BlockSpec((None, 1, D_MODEL), lambda i: (layer, 0, 0))],
        out_specs=pl.BlockSpec((tm, D_MODEL), lambda i: (i, 0)),
        out_shape=jax.ShapeDtypeStruct((TOKENS, D_MODEL), BF16),
        compiler_params=_params("arbitrary"),
        name="rmsnorm_in",
    )(x, g3)


def _in_proj_kernel(h_ref, w_ref, o_ref, wbf_ref):
    @pl.when(pl.program_id(1) == 0)
    def _():
        wbf_ref[...] = w_ref[...].astype(BF16)

    acc = jnp.dot(h_ref[...], wbf_ref[...], preferred_element_type=F32)
    o_ref[...] = acc.astype(o_ref.dtype)


def _in_proj_gate_kernel(h_ref, w_ref, b_ref, o_ref, wbf_ref):
    @pl.when(pl.program_id(1) == 0)
    def _():
        wbf_ref[...] = (0.5 * w_ref[...]).astype(BF16)

    half_z = jnp.dot(h_ref[...], wbf_ref[...], preferred_element_type=F32) + 0.5 * b_ref[...]
    o_ref[...] = (0.5 * jnp.tanh(half_z) + 0.5).astype(o_ref.dtype)


def _in_proj(h, w_in, b_gate3, layer, gates):
    tm, tn = IN_PROJ_TM, IN_PROJ_TN
    n_out = N_GATE if gates else N_MIX_IN
    col0 = (COL_GATE if gates else 0) // tn
    in_specs = [pl.BlockSpec((tm, D_MODEL), lambda j, m: (m, 0)),
                pl.BlockSpec((None, D_MODEL, tn), lambda j, m: (layer, 0, j + col0))]
    args = [h, w_in]
    if gates:
        in_specs.append(pl.BlockSpec((None, 1, tn), lambda j, m: (layer, 0, j)))
        args.append(b_gate3)
    return pl.pallas_call(
        _in_proj_gate_kernel if gates else _in_proj_kernel,
        grid=(n_out // tn, TOKENS // tm),
        in_specs=in_specs,
        out_specs=pl.BlockSpec((tm, tn), lambda j, m: (m, j)),
        out_shape=jax.ShapeDtypeStruct((TOKENS, n_out), BF16),
        scratch_shapes=[pltpu.VMEM((D_MODEL, tn), BF16)],
        compiler_params=_params("arbitrary", "arbitrary"),
        name="in_proj_gates" if gates else "in_proj_mix",
    )(*args)


LOG2E = 1.4426950408889634


def _attn_kernel(sink_ref, q_ref, kv_ref, halo_ref, o_ref, kvext_ref, bias_ref, *, layer, tq):
    n_qb = tq // BLK
    i = pl.program_id(0)
    kvext_ref[0:BLK, :] = halo_ref[...]
    kvext_ref[BLK:, :] = kv_ref[...]

    @pl.when(i == 0)
    def _():
        row = lax.broadcasted_iota(jnp.int32, (BLK, 2 * BLK), 0)
        col = lax.broadcasted_iota(jnp.int32, (BLK, 2 * BLK), 1)
        dist = row + BLK - col
        in_window = (dist >= 0) & (dist < WINDOW)
        distf = dist.astype(F32)
        for h in range(N_HEADS):
            slope = 2.0 ** (-8.0 * (h + 1) / N_HEADS)
            t = jnp.where(in_window, (-slope * LOG2E) * distf, NEG_INF)
            bias_ref[0, h] = t
            bias_ref[1, h] = jnp.where(col >= BLK, t, NEG_INF)

    lane_kv = lax.broadcasted_iota(jnp.int32, (2 * BLK, LANES), 1) < HEAD_DIM
    lane_o = lax.broadcasted_iota(jnp.int32, (BLK, LANES), 1) < HEAD_DIM
    nt_dims = (((1,), (1,)), ((), ()))

    def body(b, carry):
        r0 = pl.multiple_of(b * BLK, BLK)
        first = ((i * n_qb + b) % (SEQ // BLK) == 0).astype(jnp.int32)

        band = kvext_ref[pl.ds(r0, 2 * BLK), :]
        kt = band[:, 0:LANES]
        vt = band[:, LANES:2 * LANES]
        kt_sw = pltpu.roll(kt, HEAD_DIM, 1)
        vt_sw = pltpu.roll(vt, HEAD_DIM, 1)
        zero = jnp.zeros_like(kt)
        one = jnp.ones_like(vt)
        qb = q_ref[pl.ds(r0, BLK), :]

        for kv in range(N_KV):
            k_here, k_other = (kt, kt_sw) if kv == 0 else (kt_sw, kt)
            v_here, v_other = (vt, vt_sw) if kv == 0 else (vt_sw, vt)
            k_rhs = (jnp.where(lane_kv, k_here, zero), jnp.where(lane_kv, zero, k_other))
            v_rhs = (jnp.where(lane_kv, v_here, one), jnp.where(lane_kv, one, v_other))
            base = kv * GQA * HEAD_DIM
            q4 = jnp.concatenate(
                [qb[:, base + p * LANES: base + (p + 1) * LANES] for p in range(GQA // 2)], axis=0)
            q4 = q4 * jnp.asarray(HEAD_DIM ** -0.5 * LOG2E, BF16)

            res = []
            mx = []
            for parity in range(2):
                s = lax.dot_general(q4, k_rhs[parity], nt_dims, preferred_element_type=F32)
                probs = []
                mrow = []
                for p in range(GQA // 2):
                    h = kv * GQA + 2 * p + parity
                    sp = s[p * BLK:(p + 1) * BLK] + bias_ref[first, h]
                    m = jnp.maximum(jnp.max(sp, axis=1, keepdims=True), sink_ref[layer, h] * LOG2E)
                    probs.append(jnp.exp2(sp - m).astype(BF16))
                    mrow.append(m)
                pmat = jnp.concatenate(probs, axis=0)
                res.append(jnp.dot(pmat, v_rhs[parity], preferred_element_type=F32))
                mx.append(mrow)

            for p in range(GQA // 2):
                h_e = kv * GQA + 2 * p
                r_e = res[0][p * BLK:(p + 1) * BLK]
                r_o = res[1][p * BLK:(p + 1) * BLK]
                numer = jnp.where(lane_o, r_e, r_o)
                sums = pltpu.roll(jnp.where(lane_o, r_o, r_e), HEAD_DIM, 1)
                sink_e = jnp.exp2(sink_ref[layer, h_e] * LOG2E - mx[0][p])
                sink_o = jnp.exp2(sink_ref[layer, h_e + 1] * LOG2E - mx[1][p])
                denom = sums + jnp.where(lane_o, sink_e, sink_o)
                o_ref[pl.ds(r0, BLK), base + p * LANES: base + (p + 1) * LANES] = (numer / denom).astype(o_ref.dtype)
        return carry

    lax.fori_loop(0, n_qb, body, 0, unroll=True)


def _attention(a, sinks, layer):
    tq = ATTN_TQ
    q_blk = COL_Q // ATTN_WIDTH
    kv_blk = COL_K // (2 * KV_WIDTH)
    return pl.pallas_call(
        functools.partial(_attn_kernel, layer=layer, tq=tq),
        grid=(TOKENS // tq,),
        in_specs=[pl.BlockSpec(memory_space=pltpu.SMEM),
                  pl.BlockSpec((tq, ATTN_WIDTH), lambda i: (i, q_blk)),
                  pl.BlockSpec((tq, 2 * KV_WIDTH), lambda i: (i, kv_blk)),
                  pl.BlockSpec((BLK, 2 * KV_WIDTH), lambda i: (jnp.maximum(i * (tq // BLK) - 1, 0), kv_blk))],
        out_specs=pl.BlockSpec((tq, ATTN_WIDTH), lambda i: (i, 0)),
        out_shape=jax.ShapeDtypeStruct((TOKENS, ATTN_WIDTH), BF16),
        scratch_shapes=[pltpu.VMEM((BLK + tq, 2 * KV_WIDTH), BF16),
                        pltpu.VMEM((2, N_HEADS, BLK, 2 * BLK), F32)],
        compiler_params=_params("arbitrary"),
        name="swa_attention",
    )(sinks, a, a, a)


POOL_HALO = 16
CONV_HALO = SUBLANES_F32
C_BLK = IN_PROJ_TN
C_CHUNK = 256
MERGE_CHUNKS = 2


def _mix_kernel(u_ref, c0_ref, c1_ref, c2_ref, c3_ref, att_ref, g_ref, wgrp_ref, wpo_ref, wao_ref, wco_ref,
                scale_ref, wcm_ref, out_ref, uext_ref, zext_ref, pa_ref, pg_ref, cv_ref, *, tm):
    i = pl.program_id(0)
    seq_start = (i % (SEQ // tm)) == 0
    c_refs = (c0_ref, c1_ref, c2_ref, c3_ref)

    def conv_cols(col, width):
        blk, off = divmod(col, C_BLK)
        assert off + width <= C_BLK
        return c_refs[blk][:, off:off + width].astype(F32)

    @pl.when(seq_start)
    def _():
        uext_ref[0:POOL_HALO, :] = jnp.zeros((POOL_HALO, POOL_WIDTH), BF16)
        zext_ref[0:CONV_HALO, :] = jnp.zeros((CONV_HALO, CONV_WIDTH), F32)

    @pl.when(jnp.logical_not(seq_start))
    def _():
        uext_ref[0:POOL_HALO, :] = uext_ref[tm:tm + POOL_HALO, :]
        zext_ref[0:CONV_HALO, :] = zext_ref[tm:tm + CONV_HALO, :]

    uext_ref[POOL_HALO:, :] = u_ref[...]

    r_idx = lax.broadcasted_iota(jnp.int32, (BLK, BLK + POOL_HALO), 0)
    c_idx = lax.broadcasted_iota(jnp.int32, (BLK, BLK + POOL_HALO), 1)
    back = r_idx + POOL_HALO - c_idx
    bands = [jnp.where((back >= 0) & (back < w), 1.0, 0.0).astype(BF16) for w in POOL_WINDOWS]
    for rb in range(tm // BLK):
        pos = (i * tm + rb * BLK) % SEQ + lax.broadcasted_iota(jnp.int32, (BLK, 1), 0)
        for gi, w in enumerate(POOL_WINDOWS):
            cs = slice(gi * POOL_GROUP, (gi + 1) * POOL_GROUP)
            win = jnp.dot(bands[gi], uext_ref[rb * BLK:(rb + 1) * BLK + POOL_HALO, cs], preferred_element_type=F32)
            inv_cnt = 1.0 / jnp.minimum(pos + 1, w).astype(F32)
            tok = uext_ref[POOL_HALO + rb * BLK:POOL_HALO + (rb + 1) * BLK, cs].astype(F32)
            pa_ref[rb * BLK:(rb + 1) * BLK, cs] = (win * inv_cnt - tok).astype(BF16)
    for gi in range(len(POOL_WINDOWS)):
        cs = slice(gi * POOL_GROUP, (gi + 1) * POOL_GROUP)
        y = jnp.dot(pa_ref[:, cs], wgrp_ref[gi], preferred_element_type=F32) * scale_ref[:, cs]
        pg_ref[:, cs] = y.astype(BF16)

    for q in range(CONV_WIDTH // C_CHUNK):
        cs = slice(q * C_CHUNK, (q + 1) * C_CHUNK)
        zext_ref[CONV_HALO:, cs] = (conv_cols(CONV_WIDTH + q * C_CHUNK, C_CHUNK)
                                    * conv_cols(2 * CONV_WIDTH + q * C_CHUNK, C_CHUNK))
    for q in range(CONV_WIDTH // C_CHUNK):
        cs = slice(q * C_CHUNK, (q + 1) * C_CHUNK)
        ext = zext_ref[:, cs]
        y = (wcm_ref[0:1, cs] * pltpu.roll(ext, 2, 0)[CONV_HALO:]
             + wcm_ref[1:2, cs] * pltpu.roll(ext, 1, 0)[CONV_HALO:]
             + wcm_ref[2:3, cs] * ext[CONV_HALO:])
        cv_ref[:, cs] = (conv_cols(q * C_CHUNK, C_CHUNK) * y).astype(BF16)

    half = D_MODEL // MERGE_CHUNKS
    for nc in range(MERGE_CHUNKS):
        cs = slice(nc * half, (nc + 1) * half)
        ya = jnp.dot(pg_ref[...], wpo_ref[:, cs], preferred_element_type=F32)
        yb = jnp.dot(att_ref[...], wao_ref[:, cs], preferred_element_type=F32)
        yc = None
        for q in range(CONV_WIDTH // C_CHUNK):
            ks = slice(q * C_CHUNK, (q + 1) * C_CHUNK)
            part = jnp.dot(cv_ref[:, ks], wco_ref[ks, cs], preferred_element_type=F32)
            yc = part if yc is None else yc + part
        ga = g_ref[:, nc * half:(nc + 1) * half].astype(F32)
        gb = g_ref[:, D_MODEL + nc * half:D_MODEL + (nc + 1) * half].astype(F32)
        gc = g_ref[:, 2 * D_MODEL + nc * half:2 * D_MODEL + (nc + 1) * half].astype(F32)
        out_ref[:, cs] = (ga * ya + gb * yb + gc * yc).astype(out_ref.dtype)


def _mix(a, att, gates, wgrp, wpo, wao, wco, pool_scale3, w_conv_mix, layer):
    tm = MIX_TM
    c_blk0 = COL_CB // C_BLK
    assert COL_CB % C_BLK == 0 and (3 * CONV_WIDTH) % C_BLK == 0
    c_specs = [pl.BlockSpec((tm, C_BLK), functools.partial(lambda i, k: (i, c_blk0 + k), k=k)) for k in range(4)]
    return pl.pallas_call(
        functools.partial(_mix_kernel, tm=tm),
        grid=(TOKENS // tm,),
        in_specs=[pl.BlockSpec((tm, POOL_WIDTH), lambda i: (i, 0))] + c_specs + [
            pl.BlockSpec((tm, ATTN_WIDTH), lambda i: (i, 0)),
            pl.BlockSpec((tm, N_GATE), lambda i: (i, 0)),
            _resident((None, len(POOL_WINDOWS), POOL_GROUP, POOL_GROUP), lambda i: (layer, 0, 0, 0)),
            _resident((None, POOL_WIDTH, D_MODEL), lambda i: (layer, 0, 0)),
            _resident((None, ATTN_WIDTH, D_MODEL), lambda i: (layer, 0, 0)),
            _resident((None, CONV_WIDTH, D_MODEL), lambda i: (layer, 0, 0)),
            _resident((None, 1, POOL_WIDTH), lambda i: (layer, 0, 0)),
            _resident((None, 3, CONV_WIDTH), lambda i: (layer, 0, 0)),
        ],
        out_specs=pl.BlockSpec((tm, D_MODEL), lambda i: (i, 0)),
        out_shape=jax.ShapeDtypeStruct((TOKENS, D_MODEL), BF16),
        scratch_shapes=[pltpu.VMEM((POOL_HALO + tm, POOL_WIDTH), BF16),
                        pltpu.VMEM((CONV_HALO + tm, CONV_WIDTH), F32),
                        pltpu.VMEM((tm, POOL_WIDTH), BF16),
                        pltpu.VMEM((tm, POOL_WIDTH), BF16),
                        pltpu.VMEM((tm, CONV_WIDTH), BF16)],
        compiler_params=_params("arbitrary"),
        name="mixers_merge",
    )(a, a, a, a, a, att, gates, wgrp, wpo, wao, wco, pool_scale3, w_conv_mix)


def _proj_residual_kernel(a_ref, x_ref, w_ref, g_ref, *out_refs, final):
    x_new = x_ref[...] + jnp.dot(a_ref[...], w_ref[...], preferred_element_type=F32)
    normed = _rms_scale(x_new, g_ref[...])
    if final:
        out_refs[0][...] = normed
    else:
        out_refs[0][...] = x_new
        out_refs[1][...] = normed.astype(out_refs[1].dtype)


def _proj_residual(act, x, w, g3, w_layer, g_layer, tm, final, name):
    k = act.shape[1]
    tok_spec = pl.BlockSpec((tm, D_MODEL), lambda i: (i, 0))
    if final:
        out_specs = tok_spec
        out_shape = jax.ShapeDtypeStruct((TOKENS, D_MODEL), F32)
    else:
        out_specs = [tok_spec, tok_spec]
        out_shape = [jax.ShapeDtypeStruct((TOKENS, D_MODEL), F32), jax.ShapeDtypeStruct((TOKENS, D_MODEL), BF16)]
    g_spec = (_resident((1, D_MODEL), lambda i: (0, 0)) if g_layer is None
              else _resident((None, 1, D_MODEL), lambda i: (g_layer, 0, 0)))
    return pl.pallas_call(
        functools.partial(_proj_residual_kernel, final=final),
        grid=(TOKENS // tm,),
        in_specs=[pl.BlockSpec((tm, k), lambda i: (i, 0)),
                  tok_spec,
                  _resident((None, k, D_MODEL), lambda i: (w_layer, 0, 0)),
                  g_spec],
        out_specs=out_specs,
        out_shape=out_shape,
        compiler_params=_params("arbitrary"),
        name=name,
    )(act, x, w, g3)


def _up_kernel(h_ref, wg_ref, wv_ref, cwg_ref, cwv_ref, bg_ref, bv_ref, o_ref, wbf_ref, zext_ref, *, tm, tn):
    m = pl.program_id(1)

    @pl.when(m == 0)
    def _():
        wbf_ref[:, 0:tn] = wg_ref[...].astype(BF16)
        wbf_ref[:, tn:2 * tn] = wv_ref[...].astype(BF16)

    seq_start = (m % (SEQ // tm)) == 0

    @pl.when(seq_start)
    def _():
        zext_ref[0:CONV_HALO, :] = jnp.zeros((CONV_HALO, 2 * tn), F32)

    @pl.when(jnp.logical_not(seq_start))
    def _():
        zext_ref[0:CONV_HALO, :] = zext_ref[tm:tm + CONV_HALO, :]

    zext_ref[CONV_HALO:, :] = jnp.dot(h_ref[...], wbf_ref[...], preferred_element_type=F32)

    def conv(cs, cw_ref, b_ref, scale):
        taps = cw_ref[...] * scale
        ext = zext_ref[:, cs]
        back1 = pltpu.roll(ext, 1, 0)[CONV_HALO:]
        back2 = pltpu.roll(ext, 2, 0)[CONV_HALO:]
        return (taps[0:1] * back2 + taps[1:2] * back1 + taps[2:3] * ext[CONV_HALO:] + b_ref[...] * scale)

    half_gate = conv(slice(0, tn), cwg_ref, bg_ref, 0.5)
    u_val = conv(slice(tn, 2 * tn), cwv_ref, bv_ref, 1.0)
    o_ref[...] = (half_gate * (1.0 + jnp.tanh(half_gate)) * u_val).astype(o_ref.dtype)


def _up(h2, w_up, w_ffn_conv, b_ffn_conv3, layer):
    tm, tn = UP_TM, UP_TN
    nj = D_FF // tn
    w_spec = lambda off: pl.BlockSpec((None, D_MODEL, tn), lambda j, m: (layer, 0, j + off),
                                      pipeline_mode=pl.Buffered(1))
    return pl.pallas_call(
        functools.partial(_up_kernel, tm=tm, tn=tn),
        grid=(nj, TOKENS // tm),
        in_specs=[pl.BlockSpec((tm, D_MODEL), lambda j, m: (m, 0)),
                  w_spec(0),
                  w_spec(nj),
                  pl.BlockSpec((None, 3, tn), lambda j, m: (layer, 0, j)),
                  pl.BlockSpec((None, 3, tn), lambda j, m: (layer, 0, j + nj)),
                  pl.BlockSpec((None, 1, tn), lambda j, m: (layer, 0, j)),
                  pl.BlockSpec((None, 1, tn), lambda j, m: (layer, 0, j + nj))],
        out_specs=pl.BlockSpec((tm, tn), lambda j, m: (m, j)),
        out_shape=jax.ShapeDtypeStruct((TOKENS, D_FF), BF16),
        scratch_shapes=[pltpu.VMEM((D_MODEL, 2 * tn), BF16),
                        pltpu.VMEM((CONV_HALO + tm, 2 * tn), F32)],
        compiler_params=_params("arbitrary", "arbitrary"),
        name="up_conv_act",
    )(h2, w_up, w_up, w_ffn_conv, w_ffn_conv, b_ffn_conv3, b_ffn_conv3)


def kernel(x, g_mix, w_in, b_gate, w_pool_grp, pool_scale, w_pool_out, sinks, w_attn_out, w_conv_mix, w_conv_out,
           w_o, g_ffn, w_up, w_ffn_conv, b_ffn_conv, w_down, g_final):
    assert x.shape == (BATCH, SEQ, D_MODEL) and x.dtype == F32
    xt = x.reshape(TOKENS, D_MODEL)
    g_mix3 = g_mix.reshape(DEPTH, 1, D_MODEL)
    g_ffn3 = g_ffn.reshape(DEPTH, 1, D_MODEL)
    g_final2 = g_final.reshape(1, D_MODEL)
    b_gate3 = b_gate.reshape(DEPTH, 1, N_GATE)
    pool_scale3 = pool_scale.reshape(DEPTH, 1, POOL_WIDTH)
    b_ffn3 = b_ffn_conv.reshape(DEPTH, 1, 2 * D_FF)
    wgrp, wpo, wao, wco, wo, wdn = (w.astype(BF16) for w in (w_pool_grp, w_pool_out, w_attn_out, w_conv_out, w_o, w_down))

    h = _rmsnorm_bf16(xt, g_mix3, 0)
    for layer in range(DEPTH):
        a = _in_proj(h, w_in, None, layer, gates=False)
        gates = _in_proj(h, w_in, b_gate3, layer, gates=True)
        att = _attention(a, sinks, layer)
        merged = _mix(a, att, gates, wgrp, wpo, wao, wco, pool_scale3, w_conv_mix, layer)
        xt, h2 = _proj_residual(merged, xt, wo, g_ffn3, layer, layer, OPROJ_TM, False, "out_proj_norm")
        act = _up(h2, w_up, w_ffn_conv, b_ffn3, layer)
        if layer + 1 < DEPTH:
            xt, h = _proj_residual(act, xt, wdn, g_mix3, layer, layer + 1, DOWN_TM, False, "down_proj_norm")
        else:
            out = _proj_residual(act, xt, wdn, g_final2, layer, None, DOWN_TM, True, "down_proj_final")
    return out.reshape(BATCH, SEQ, D_MODEL)
```

```python
import functools

import jax
import jax.numpy as jnp
from jax import lax
from jax.experimental import pallas as pl
from jax.experimental.pallas import tpu as pltpu

D_MODEL = 2048
BATCH = 2
SEQ = 16384
DEPTH = 2
TOKENS = BATCH * SEQ

POOL_WIDTH = D_MODEL // 2
POOL_WINDOWS = (2, 4, 8, 16)
POOL_GROUP = POOL_WIDTH // len(POOL_WINDOWS)
HEAD_DIM = 64
N_HEADS = D_MODEL // 128
GQA = 8
N_KV = N_HEADS // GQA
WINDOW = 128
BLK = 128
CONV_WIDTH = D_MODEL // 2
D_FF = ((8 * D_MODEL // 3 + 255) // 256) * 256
N_BRANCH = 3
RMS_EPS = 1e-6
NEG_INF = -1e30

ATTN_WIDTH = N_HEADS * HEAD_DIM
KV_WIDTH = N_KV * HEAD_DIM
COL_U = 0
COL_Q = COL_U + POOL_WIDTH
COL_K = COL_Q + ATTN_WIDTH
COL_V = COL_K + KV_WIDTH
COL_CB = COL_V + KV_WIDTH
COL_CC = COL_CB + CONV_WIDTH
COL_CH = COL_CC + CONV_WIDTH
COL_GATE = COL_CH + CONV_WIDTH
N_MIX_IN = COL_GATE
N_GATE = N_BRANCH * D_MODEL

LANES = 128
SUBLANES_F32 = 8
VMEM_LIMIT_BYTES = 56 * 1024 * 1024

BF16 = jnp.bfloat16
F32 = jnp.float32

IN_PROJ_TN = 768
IN_PROJ_TM = 2048
ATTN_TQ = 1024
MIX_TM = 512
OPROJ_TM = 512
UP_TM = 512
UP_TN = 1408
DOWN_TM = 256
DOWN_FINAL_TM = 512
NORM_TM = 1024


def _params(*sem):
    return pltpu.CompilerParams(dimension_semantics=sem, vmem_limit_bytes=VMEM_LIMIT_BYTES)


def _resident(shape, index_map):
    return pl.BlockSpec(shape, index_map, pipeline_mode=pl.Buffered(1))


def _rms_scale(x, g):
    ms = jnp.mean(x * x, axis=-1, keepdims=True)
    return x * lax.rsqrt(ms + RMS_EPS) * g


def _rmsnorm_kernel(x_ref, g_ref, o_ref):
    o_ref[...] = _rms_scale(x_ref[...], g_ref[...]).astype(o_ref.dtype)


def _rmsnorm_bf16(x, g3, layer):
    tm = NORM_TM
    return pl.pallas_call(
        _rmsnorm_kernel,
        grid=(TOKENS // tm,),
        in_specs=[pl.BlockSpec((tm, D_MODEL), lambda i: (i, 0)),
                  pl.BlockSpec((None, 1, D_MODEL), lambda i: (layer, 0, 0))],
        out_specs=pl.BlockSpec((tm, D_MODEL), lambda i: (i, 0)),
        out_shape=jax.ShapeDtypeStruct((TOKENS, D_MODEL), BF16),
        compiler_params=_params("arbitrary"),
        name="rmsnorm_in",
    )(x, g3)


def _in_proj_kernel(h_ref, w_ref, o_ref, wbf_ref):
    @pl.when(pl.program_id(1) == 0)
    def _():
        wbf_ref[...] = w_ref[...].astype(BF16)

    acc = jnp.dot(h_ref[...], wbf_ref[...], preferred_element_type=F32)
    o_ref[...] = acc.astype(o_ref.dtype)


def _in_proj_gate_kernel(h_ref, w_ref, b_ref, o_ref, wbf_ref):
    @pl.when(pl.program_id(1) == 0)
    def _():
        wbf_ref[...] = (0.5 * w_ref[...]).astype(BF16)

    half_z = jnp.dot(h_ref[...], wbf_ref[...], preferred_element_type=F32) + 0.5 * b_ref[...]
    o_ref[...] = (0.5 * jnp.tanh(half_z) + 0.5).astype(o_ref.dtype)


def _in_proj(h, w_in, b_gate3, layer, gates):
    tm, tn = IN_PROJ_TM, IN_PROJ_TN
    n_out = N_GATE if gates else N_MIX_IN
    col0 = (COL_GATE if gates else 0) // tn
    in_specs = [pl.BlockSpec((tm, D_MODEL), lambda j, m: (m, 0)),
                pl.BlockSpec((None, D_MODEL, tn), lambda j, m: (layer, 0, j + col0))]
    args = [h, w_in]
    if gates:
        in_specs.append(pl.BlockSpec((None, 1, tn), lambda j, m: (layer, 0, j)))
        args.append(b_gate3)
    return pl.pallas_call(
        _in_proj_gate_kernel if gates else _in_proj_kernel,
        grid=(n_out // tn, TOKENS // tm),
        in_specs=in_specs,
        out_specs=pl.BlockSpec((tm, tn), lambda j, m: (m, j)),
        out_shape=jax.ShapeDtypeStruct((TOKENS, n_out), BF16),
        scratch_shapes=[pltpu.VMEM((D_MODEL, tn), BF16)],
        compiler_params=_params("arbitrary", "arbitrary"),
        name="in_proj_gates" if gates else "in_proj_mix",
    )(*args)


LOG2E = 1.4426950408889634


def _attn_kernel(sink_ref, q_ref, kv_ref, halo_ref, o_ref, kvext_ref, bias_ref, *, layer, tq):
    n_qb = tq // BLK
    i = pl.program_id(0)
    kvext_ref[0:BLK, :] = halo_ref[...]
    kvext_ref[BLK:, :] = kv_ref[...]

    @pl.when(i == 0)
    def _():
        row = lax.broadcasted_iota(jnp.int32, (BLK, 2 * BLK), 0)
        col = lax.broadcasted_iota(jnp.int32, (BLK, 2 * BLK), 1)
        dist = row + BLK - col
        in_window = (dist >= 0) & (dist < WINDOW)
        distf = dist.astype(F32)
        for h in range(N_HEADS):
            slope = 2.0 ** (-8.0 * (h + 1) / N_HEADS)
            t = jnp.where(in_window, (-slope * LOG2E) * distf, NEG_INF)
            bias_ref[0, h] = t
            bias_ref[1, h] = jnp.where(col >= BLK, t, NEG_INF)

    lane_kv = lax.broadcasted_iota(jnp.int32, (2 * BLK, LANES), 1) < HEAD_DIM
    lane_o = lax.broadcasted_iota(jnp.int32, (BLK, LANES), 1) < HEAD_DIM
    nt_dims = (((1,), (1,)), ((), ()))

    def body(b, carry):
        r0 = pl.multiple_of(b * BLK, BLK)
        first = ((i * n_qb + b) % (SEQ // BLK) == 0).astype(jnp.int32)

        band = kvext_ref[pl.ds(r0, 2 * BLK), :]
        kt = band[:, 0:LANES]
        vt = band[:, LANES:2 * LANES]
        kt_sw = pltpu.roll(kt, HEAD_DIM, 1)
        vt_sw = pltpu.roll(vt, HEAD_DIM, 1)
        zero = jnp.zeros_like(kt)
        one = jnp.ones_like(vt)
        qb = q_ref[pl.ds(r0, BLK), :]

        for kv in range(N_KV):
            k_here, k_other = (kt, kt_sw) if kv == 0 else (kt_sw, kt)
            v_here, v_other = (vt, vt_sw) if kv == 0 else (vt_sw, vt)
            k_rhs = (jnp.where(lane_kv, k_here, zero), jnp.where(lane_kv, zero, k_other))
            v_rhs = (jnp.where(lane_kv, v_here, one), jnp.where(lane_kv, one, v_other))
            base = kv * GQA * HEAD_DIM
            q4 = jnp.concatenate(
                [qb[:, base + p * LANES: base + (p + 1) * LANES] for p in range(GQA // 2)], axis=0)
            q4 = q4 * jnp.asarray(HEAD_DIM ** -0.5 * LOG2E, BF16)

            res = []
            mx = []
            for parity in range(2):
                s = lax.dot_general(q4, k_rhs[parity], nt_dims, preferred_element_type=F32)
                probs = []
                mrow = []
                for p in range(GQA // 2):
                    h = kv * GQA + 2 * p + parity
                    sp = s[p * BLK:(p + 1) * BLK] + bias_ref[first, h]
                    m = jnp.maximum(jnp.max(sp, axis=1, keepdims=True), sink_ref[layer, h] * LOG2E)
                    probs.append(jnp.exp2(sp - m).astype(BF16))
                    mrow.append(m)
                pmat = jnp.concatenate(probs, axis=0)
                res.append(jnp.dot(pmat, v_rhs[parity], preferred_element_type=F32))
                mx.append(mrow)

            for p in range(GQA // 2):
                h_e = kv * GQA + 2 * p
                r_e = res[0][p * BLK:(p + 1) * BLK]
                r_o = res[1][p * BLK:(p + 1) * BLK]
                numer = jnp.where(lane_o, r_e, r_o)
                sums = pltpu.roll(jnp.where(lane_o, r_o, r_e), HEAD_DIM, 1)
                sink_e = jnp.exp2(sink_ref[layer, h_e] * LOG2E - mx[0][p])
                sink_o = jnp.exp2(sink_ref[layer, h_e + 1] * LOG2E - mx[1][p])
                denom = sums + jnp.where(lane_o, sink_e, sink_o)
                o_ref[pl.ds(r0, BLK), base + p * LANES: base + (p + 1) * LANES] = (numer / denom).astype(o_ref.dtype)
        return carry

    lax.fori_loop(0, n_qb, body, 0, unroll=True)


def _attention(a, sinks, layer):
    tq = ATTN_TQ
    q_blk = COL_Q // ATTN_WIDTH
    kv_blk = COL_K // (2 * KV_WIDTH)
    return pl.pallas_call(
        functools.partial(_attn_kernel, layer=layer, tq=tq),
        grid=(TOKENS // tq,),
        in_specs=[pl.BlockSpec(memory_space=pltpu.SMEM),
                  pl.BlockSpec((tq, ATTN_WIDTH), lambda i: (i, q_blk)),
                  pl.BlockSpec((tq, 2 * KV_WIDTH), lambda i: (i, kv_blk)),
                  pl.BlockSpec((BLK, 2 * KV_WIDTH), lambda i: (jnp.maximum(i * (tq // BLK) - 1, 0), kv_blk))],
        out_specs=pl.BlockSpec((tq, ATTN_WIDTH), lambda i: (i, 0)),
        out_shape=jax.ShapeDtypeStruct((TOKENS, ATTN_WIDTH), BF16),
        scratch_shapes=[pltpu.VMEM((BLK + tq, 2 * KV_WIDTH), BF16),
                        pltpu.VMEM((2, N_HEADS, BLK, 2 * BLK), F32)],
        compiler_params=_params("arbitrary"),
        name="swa_attention",
    )(sinks, a, a, a)


POOL_HALO = 16
CONV_HALO = SUBLANES_F32
C_BLK = IN_PROJ_TN
C_CHUNK = 256
MERGE_CHUNKS = 2


def _mix_kernel(u_ref, c0_ref, c1_ref, c2_ref, c3_ref, att_ref, g_ref, wgrp_ref, wpo_ref, wao_ref, wco_ref,
                scale_ref, wcm_ref, out_ref, uext_ref, zext_ref, pa_ref, pg_ref, cv_ref, *, tm):
    i = pl.program_id(0)
    seq_start = (i % (SEQ // tm)) == 0
    c_refs = (c0_ref, c1_ref, c2_ref, c3_ref)

    def conv_cols(col, width):
        blk, off = divmod(col, C_BLK)
        assert off + width <= C_BLK
        return c_refs[blk][:, off:off + width].astype(F32)

    @pl.when(seq_start)
    def _():
        uext_ref[0:POOL_HALO, :] = jnp.zeros((POOL_HALO, POOL_WIDTH), BF16)
        zext_ref[0:CONV_HALO, :] = jnp.zeros((CONV_HALO, CONV_WIDTH), F32)

    @pl.when(jnp.logical_not(seq_start))
    def _():
        uext_ref[0:POOL_HALO, :] = uext_ref[tm:tm + POOL_HALO, :]
        zext_ref[0:CONV_HALO, :] = zext_ref[tm:tm + CONV_HALO, :]

    uext_ref[POOL_HALO:, :] = u_ref[...]

    r_idx = lax.broadcasted_iota(jnp.int32, (BLK, BLK + POOL_HALO), 0)
    c_idx = lax.broadcasted_iota(jnp.int32, (BLK, BLK + POOL_HALO), 1)
    back = r_idx + POOL_HALO - c_idx
    bands = [jnp.where((back >= 0) & (back < w), 1.0, 0.0).astype(BF16) for w in POOL_WINDOWS]
    for rb in range(tm // BLK):
        pos = (i * tm + rb * BLK) % SEQ + lax.broadcasted_iota(jnp.int32, (BLK, 1), 0)
        for gi, w in enumerate(POOL_WINDOWS):
            cs = slice(gi * POOL_GROUP, (gi + 1) * POOL_GROUP)
            win = jnp.dot(bands[gi], uext_ref[rb * BLK:(rb + 1) * BLK + POOL_HALO, cs], preferred_element_type=F32)
            inv_cnt = 1.0 / jnp.minimum(pos + 1, w).astype(F32)
            tok = uext_ref[POOL_HALO + rb * BLK:POOL_HALO + (rb + 1) * BLK, cs].astype(F32)
            pa_ref[rb * BLK:(rb + 1) * BLK, cs] = (win * inv_cnt - tok).astype(BF16)
    for gi in range(len(POOL_WINDOWS)):
        cs = slice(gi * POOL_GROUP, (gi + 1) * POOL_GROUP)
        y = jnp.dot(pa_ref[:, cs], wgrp_ref[gi], preferred_element_type=F32) * scale_ref[:, cs]
        pg_ref[:, cs] = y.astype(BF16)

    for q in range(CONV_WIDTH // C_CHUNK):
        cs = slice(q * C_CHUNK, (q + 1) * C_CHUNK)
        zext_ref[CONV_HALO:, cs] = (conv_cols(CONV_WIDTH + q * C_CHUNK, C_CHUNK)
                                    * conv_cols(2 * CONV_WIDTH + q * C_CHUNK, C_CHUNK))
    for q in range(CONV_WIDTH // C_CHUNK):
        cs = slice(q * C_CHUNK, (q + 1) * C_CHUNK)
        ext = zext_ref[:, cs]
        y = (wcm_ref[0:1, cs] * pltpu.roll(ext, 2, 0)[CONV_HALO:]
             + wcm_ref[1:2, cs] * pltpu.roll(ext, 1, 0)[CONV_HALO:]
             + wcm_ref[2:3, cs] * ext[CONV_HALO:])
        cv_ref[:, cs] = (conv_cols(q * C_CHUNK, C_CHUNK) * y).astype(BF16)

    half = D_MODEL // MERGE_CHUNKS
    for nc in range(MERGE_CHUNKS):
        cs = slice(nc * half, (nc + 1) * half)
        ya = jnp.dot(pg_ref[...], wpo_ref[:, cs], preferred_element_type=F32)
        yb = jnp.dot(att_ref[...], wao_ref[:, cs], preferred_element_type=F32)
        yc = None
        for q in range(CONV_WIDTH // C_CHUNK):
            ks = slice(q * C_CHUNK, (q + 1) * C_CHUNK)
            part = jnp.dot(cv_ref[:, ks], wco_ref[ks, cs], preferred_element_type=F32)
            yc = part if yc is None else yc + part
        ga = g_ref[:, nc * half:(nc + 1) * half].astype(F32)
        gb = g_ref[:, D_MODEL + nc * half:D_MODEL + (nc + 1) * half].astype(F32)
        gc = g_ref[:, 2 * D_MODEL + nc * half:2 * D_MODEL + (nc + 1) * half].astype(F32)
        out_ref[:, cs] = (ga * ya + gb * yb + gc * yc).astype(out_ref.dtype)


def _mix(a, att, gates, wgrp, wpo, wao, wco, pool_scale3, w_conv_mix, layer):
    tm = MIX_TM
    c_blk0 = COL_CB // C_BLK
    assert COL_CB % C_BLK == 0 and (3 * CONV_WIDTH) % C_BLK == 0
    c_specs = [pl.BlockSpec((tm, C_BLK), functools.partial(lambda i, k: (i, c_blk0 + k), k=k)) for k in range(4)]
    return pl.pallas_call(
        functools.partial(_mix_kernel, tm=tm),
        grid=(TOKENS // tm,),
        in_specs=[pl.BlockSpec((tm, POOL_WIDTH), lambda i: (i, 0))] + c_specs + [
            pl.BlockSpec((tm, ATTN_WIDTH), lambda i: (i, 0)),
            pl.BlockSpec((tm, N_GATE), lambda i: (i, 0)),
            _resident((None, len(POOL_WINDOWS), POOL_GROUP, POOL_GROUP), lambda i: (layer, 0, 0, 0)),
            _resident((None, POOL_WIDTH, D_MODEL), lambda i: (layer, 0, 0)),
            _resident((None, ATTN_WIDTH, D_MODEL), lambda i: (layer, 0, 0)),
            _resident((None, CONV_WIDTH, D_MODEL), lambda i: (layer, 0, 0)),
            _resident((None, 1, POOL_WIDTH), lambda i: (layer, 0, 0)),
            _resident((None, 3, CONV_WIDTH), lambda i: (layer, 0, 0)),
        ],
        out_specs=pl.BlockSpec((tm, D_MODEL), lambda i: (i, 0)),
        out_shape=jax.ShapeDtypeStruct((TOKENS, D_MODEL), BF16),
        scratch_shapes=[pltpu.VMEM((POOL_HALO + tm, POOL_WIDTH), BF16),
                        pltpu.VMEM((CONV_HALO + tm, CONV_WIDTH), F32),
                        pltpu.VMEM((tm, POOL_WIDTH), BF16),
                        pltpu.VMEM((tm, POOL_WIDTH), BF16),
                        pltpu.VMEM((tm, CONV_WIDTH), BF16)],
        compiler_params=_params("arbitrary"),
        name="mixers_merge",
    )(a, a, a, a, a, att, gates, wgrp, wpo, wao, wco, pool_scale3, w_conv_mix)


def _proj_residual_kernel(a_ref, x_ref, w_ref, g_ref, *out_refs, final):
    x_new = x_ref[...] + jnp.dot(a_ref[...], w_ref[...], preferred_element_type=F32)
    normed = _rms_scale(x_new, g_ref[...])
    if final:
        out_refs[0][...] = normed
    else:
        out_refs[0][...] = x_new
        out_refs[1][...] = normed.astype(out_refs[1].dtype)


def _proj_residual(act, x, w, g3, w_layer, g_layer, tm, final, name):
    k = act.shape[1]
    tok_spec = pl.BlockSpec((tm, D_MODEL), lambda i: (i, 0))
    if final:
        out_specs = tok_spec
        out_shape = jax.ShapeDtypeStruct((TOKENS, D_MODEL), F32)
    else:
        out_specs = [tok_spec, tok_spec]
        out_shape = [jax.ShapeDtypeStruct((TOKENS, D_MODEL), F32), jax.ShapeDtypeStruct((TOKENS, D_MODEL), BF16)]
    g_spec = (_resident((1, D_MODEL), lambda i: (0, 0)) if g_layer is None
              else _resident((None, 1, D_MODEL), lambda i: (g_layer, 0, 0)))
    return pl.pallas_call(
        functools.partial(_proj_residual_kernel, final=final),
        grid=(TOKENS // tm,),
        in_specs=[pl.BlockSpec((tm, k), lambda i: (i, 0)),
                  tok_spec,
                  _resident((None, k, D_MODEL), lambda i: (w_layer, 0, 0)),
                  g_spec],
        out_specs=out_specs,
        out_shape=out_shape,
        compiler_params=_params("arbitrary"),
        name=name,
    )(act, x, w, g3)


def _up_kernel(h_ref, wg_ref, wv_ref, cwg_ref, cwv_ref, bg_ref, bv_ref, o_ref, wbf_ref, zext_ref, *, tm, tn):
    m = pl.program_id(1)

    @pl.when(m == 0)
    def _():
        wbf_ref[:, 0:tn] = wg_ref[...].astype(BF16)
        wbf_ref[:, tn:2 * tn] = wv_ref[...].astype(BF16)

    seq_start = (m % (SEQ // tm)) == 0

    @pl.when(seq_start)
    def _():
        zext_ref[0:CONV_HALO, :] = jnp.zeros((CONV_HALO, 2 * tn), F32)

    @pl.when(jnp.logical_not(seq_start))
    def _():
        zext_ref[0:CONV_HALO, :] = zext_ref[tm:tm + CONV_HALO, :]

    zext_ref[CONV_HALO:, :] = jnp.dot(h_ref[...], wbf_ref[...], preferred_element_type=F32)

    def conv(cs, cw_ref, b_ref, scale):
        taps = cw_ref[...] * scale
        ext = zext_ref[:, cs]
        back1 = pltpu.roll(ext, 1, 0)[CONV_HALO:]
        back2 = pltpu.roll(ext, 2, 0)[CONV_HALO:]
        return (taps[0:1] * back2 + taps[1:2] * back1 + taps[2:3] * ext[CONV_HALO:] + b_ref[...] * scale)

    half_gate = conv(slice(0, tn), cwg_ref, bg_ref, 0.5)
    u_val = conv(slice(tn, 2 * tn), cwv_ref, bv_ref, 1.0)
    o_ref[...] = (half_gate * (1.0 + jnp.tanh(half_gate)) * u_val).astype(o_ref.dtype)


def _up(h2, w_up, w_ffn_conv, b_ffn_conv3, layer):
    tm, tn = UP_TM, UP_TN
    nj = D_FF // tn
    w_spec = lambda off: pl.BlockSpec((None, D_MODEL, tn), lambda j, m: (layer, 0, j + off),
                                      pipeline_mode=pl.Buffered(1))
    return pl.pallas_call(
        functools.partial(_up_kernel, tm=tm, tn=tn),
        grid=(nj, TOKENS // tm),
        in_specs=[pl.BlockSpec((tm, D_MODEL), lambda j, m: (m, 0)),
                  w_spec(0),
                  w_spec(nj),
                  pl.BlockSpec((None, 3, tn), lambda j, m: (layer, 0, j)),
                  pl.BlockSpec((None, 3, tn), lambda j, m: (layer, 0, j + nj)),
                  pl.BlockSpec((None, 1, tn), lambda j, m: (layer, 0, j)),
                  pl.BlockSpec((None, 1, tn), lambda j, m: (layer, 0, j + nj))],
        out_specs=pl.BlockSpec((tm, tn), lambda j, m: (m, j)),
        out_shape=jax.ShapeDtypeStruct((TOKENS, D_FF), BF16),
        scratch_shapes=[pltpu.VMEM((D_MODEL, 2 * tn), BF16),
                        pltpu.VMEM((CONV_HALO + tm, 2 * tn), F32)],
        compiler_params=_params("arbitrary", "arbitrary"),
        name="up_conv_act",
    )(h2, w_up, w_up, w_ffn_conv, w_ffn_conv, b_ffn_conv3, b_ffn_conv3)


def kernel(x, g_mix, w_in, b_gate, w_pool_grp, pool_scale, w_pool_out, sinks, w_attn_out, w_conv_mix, w_conv_out,
           w_o, g_ffn, w_up, w_ffn_conv, b_ffn_conv, w_down, g_final):
    assert x.shape == (BATCH, SEQ, D_MODEL) and x.dtype == F32
    xt = x.reshape(TOKENS, D_MODEL)
    g_mix3 = g_mix.reshape(DEPTH, 1, D_MODEL)
    g_ffn3 = g_ffn.reshape(DEPTH, 1, D_MODEL)
    g_final2 = g_final.reshape(1, D_MODEL)
    b_gate3 = b_gate.reshape(DEPTH, 1, N_GATE)
    pool_scale3 = pool_scale.reshape(DEPTH, 1, POOL_WIDTH)
    b_ffn3 = b_ffn_conv.reshape(DEPTH, 1, 2 * D_FF)
    wgrp, wpo, wao, wco, wo, wdn = (w.astype(BF16) for w in (w_pool_grp, w_pool_out, w_attn_out, w_conv_out, w_o, w_down))

    h = _rmsnorm_bf16(xt, g_mix3, 0)
    for layer in range(DEPTH):
        a = _in_proj(h, w_in, None, layer, gates=False)
        gates = _in_proj(h, w_in, b_gate3, layer, gates=True)
        att = _attention(a, sinks, layer)
        merged = _mix(a, att, gates, wgrp, wpo, wao, wco, pool_scale3, w_conv_mix, layer)
        xt, h2 = _proj_residual(merged, xt, wo, g_ffn3, layer, layer, OPROJ_TM, False, "out_proj_norm")
        act = _up(h2, w_up, w_ffn_conv, b_ffn3, layer)
        if layer + 1 < DEPTH:
            xt, h = _proj_residual(act, xt, wdn, g_mix3, layer, layer + 1, DOWN_TM, False, "down_proj_norm")
        else:
            out = _proj_residual(act, xt, wdn, g_final2, layer, None, DOWN_FINAL_TM, True, "down_proj_final")
    return out.reshape(BATCH, SEQ, D_MODEL)
```

```python
import functools

import jax
import jax.numpy as jnp
from jax import lax
from jax.experimental import pallas as pl
from jax.experimental.pallas import tpu as pltpu

D_MODEL = 2048
BATCH = 2
SEQ = 16384
DEPTH = 2
TOKENS = BATCH * SEQ

POOL_WIDTH = D_MODEL // 2
POOL_WINDOWS = (2, 4, 8, 16)
POOL_GROUP = POOL_WIDTH // len(POOL_WINDOWS)
HEAD_DIM = 64
N_HEADS = D_MODEL // 128
GQA = 8
N_KV = N_HEADS // GQA
WINDOW = 128
BLK = 128
CONV_WIDTH = D_MODEL // 2
D_FF = ((8 * D_MODEL // 3 + 255) // 256) * 256
N_BRANCH = 3
RMS_EPS = 1e-6
NEG_INF = -1e30

ATTN_WIDTH = N_HEADS * HEAD_DIM
KV_WIDTH = N_KV * HEAD_DIM
COL_U = 0
COL_Q = COL_U + POOL_WIDTH
COL_K = COL_Q + ATTN_WIDTH
COL_V = COL_K + KV_WIDTH
COL_CB = COL_V + KV_WIDTH
COL_CC = COL_CB + CONV_WIDTH
COL_CH = COL_CC + CONV_WIDTH
COL_GATE = COL_CH + CONV_WIDTH
N_MIX_IN = COL_GATE
N_GATE = N_BRANCH * D_MODEL

LANES = 128
SUBLANES_F32 = 8
VMEM_LIMIT_BYTES = 56 * 1024 * 1024

BF16 = jnp.bfloat16
F32 = jnp.float32

IN_PROJ_TN = 768
IN_PROJ_TM = 2048
ATTN_TQ = 1024
MIX_TM = 512
OPROJ_TM = 512
UP_TM = 512
UP_TN = 1408
DOWN_TM = 512
DOWN_FINAL_TM = 512
NORM_TM = 1024


def _params(*sem):
    return pltpu.CompilerParams(dimension_semantics=sem, vmem_limit_bytes=VMEM_LIMIT_BYTES)


def _resident(shape, index_map):
    return pl.BlockSpec(shape, index_map, pipeline_mode=pl.Buffered(1))


def _rms_scale(x, g):
    ms = jnp.mean(x * x, axis=-1, keepdims=True)
    return x * lax.rsqrt(ms + RMS_EPS) * g


def _rmsnorm_kernel(x_ref, g_ref, o_ref):
    o_ref[...] = _rms_scale(x_ref[...], g_ref[...]).astype(o_ref.dtype)


def _rmsnorm_bf16(x, g3, layer):
    tm = NORM_TM
    return pl.pallas_call(
        _rmsnorm_kernel,
        grid=(TOKENS // tm,),
        in_specs=[pl.BlockSpec((tm, D_MODEL), lambda i: (i, 0)),
                  pl.BlockSpec((None, 1, D_MODEL), lambda i: (layer, 0, 0))],
        out_specs=pl.BlockSpec((tm, D_MODEL), lambda i: (i, 0)),
        out_shape=jax.ShapeDtypeStruct((TOKENS, D_MODEL), BF16),
        compiler_params=_params("arbitrary"),
        name="rmsnorm_in",
    )(x, g3)


def _in_proj_kernel(h_ref, w_ref, o_ref, wbf_ref):
    @pl.when(pl.program_id(1) == 0)
    def _():
        wbf_ref[...] = w_ref[...].astype(BF16)

    acc = jnp.dot(h_ref[...], wbf_ref[...], preferred_element_type=F32)
    o_ref[...] = acc.astype(o_ref.dtype)


def _in_proj_gate_kernel(h_ref, w_ref, b_ref, o_ref, wbf_ref):
    @pl.when(pl.program_id(1) == 0)
    def _():
        wbf_ref[...] = (0.5 * w_ref[...]).astype(BF16)

    half_z = jnp.dot(h_ref[...], wbf_ref[...], preferred_element_type=F32) + 0.5 * b_ref[...]
    o_ref[...] = (0.5 * jnp.tanh(half_z) + 0.5).astype(o_ref.dtype)


def _in_proj(h, w_in, b_gate3, layer, gates):
    tm, tn = IN_PROJ_TM, IN_PROJ_TN
    n_out = N_GATE if gates else N_MIX_IN
    col0 = (COL_GATE if gates else 0) // tn
    in_specs = [pl.BlockSpec((tm, D_MODEL), lambda j, m: (m, 0)),
                pl.BlockSpec((None, D_MODEL, tn), lambda j, m: (layer, 0, j + col0))]
    args = [h, w_in]
    if gates:
        in_specs.append(pl.BlockSpec((None, 1, tn), lambda j, m: (layer, 0, j)))
        args.append(b_gate3)
    return pl.pallas_call(
        _in_proj_gate_kernel if gates else _in_proj_kernel,
        grid=(n_out // tn, TOKENS // tm),
        in_specs=in_specs,
        out_specs=pl.BlockSpec((tm, tn), lambda j, m: (m, j)),
        out_shape=jax.ShapeDtypeStruct((TOKENS, n_out), BF16),
        scratch_shapes=[pltpu.VMEM((D_MODEL, tn), BF16)],
        compiler_params=_params("arbitrary", "arbitrary"),
        name="in_proj_gates" if gates else "in_proj_mix",
    )(*args)


LOG2E = 1.4426950408889634


def _attn_kernel(sink_ref, q_ref, kv_ref, halo_ref, o_ref, kvext_ref, bias_ref, *, layer, tq):
    n_qb = tq // BLK
    i = pl.program_id(0)
    kvext_ref[0:BLK, :] = halo_ref[...]
    kvext_ref[BLK:, :] = kv_ref[...]

    @pl.when(i == 0)
    def _():
        row = lax.broadcasted_iota(jnp.int32, (BLK, 2 * BLK), 0)
        col = lax.broadcasted_iota(jnp.int32, (BLK, 2 * BLK), 1)
        dist = row + BLK - col
        in_window = (dist >= 0) & (dist < WINDOW)
        distf = dist.astype(F32)
        for h in range(N_HEADS):
            slope = 2.0 ** (-8.0 * (h + 1) / N_HEADS)
            t = jnp.where(in_window, (-slope * LOG2E) * distf, NEG_INF)
            bias_ref[0, h] = t
            bias_ref[1, h] = jnp.where(col >= BLK, t, NEG_INF)

    lane_kv = lax.broadcasted_iota(jnp.int32, (2 * BLK, LANES), 1) < HEAD_DIM
    lane_o = lax.broadcasted_iota(jnp.int32, (BLK, LANES), 1) < HEAD_DIM
    nt_dims = (((1,), (1,)), ((), ()))

    def body(b, carry):
        r0 = pl.multiple_of(b * BLK, BLK)
        first = ((i * n_qb + b) % (SEQ // BLK) == 0).astype(jnp.int32)

        band = kvext_ref[pl.ds(r0, 2 * BLK), :]
        kt = band[:, 0:LANES]
        vt = band[:, LANES:2 * LANES]
        kt_sw = pltpu.roll(kt, HEAD_DIM, 1)
        vt_sw = pltpu.roll(vt, HEAD_DIM, 1)
        zero = jnp.zeros_like(kt)
        one = jnp.ones_like(vt)
        qb = q_ref[pl.ds(r0, BLK), :]

        for kv in range(N_KV):
            k_here, k_other = (kt, kt_sw) if kv == 0 else (kt_sw, kt)
            v_here, v_other = (vt, vt_sw) if kv == 0 else (vt_sw, vt)
            k_rhs = (jnp.where(lane_kv, k_here, zero), jnp.where(lane_kv, zero, k_other))
            v_rhs = (jnp.where(lane_kv, v_here, one), jnp.where(lane_kv, one, v_other))
            base = kv * GQA * HEAD_DIM
            q4 = jnp.concatenate(
                [qb[:, base + p * LANES: base + (p + 1) * LANES] for p in range(GQA // 2)], axis=0)
            q4 = q4 * jnp.asarray(HEAD_DIM ** -0.5 * LOG2E, BF16)

            res = []
            mx = []
            for parity in range(2):
                s = lax.dot_general(q4, k_rhs[parity], nt_dims, preferred_element_type=F32)
                probs = []
                mrow = []
                for p in range(GQA // 2):
                    h = kv * GQA + 2 * p + parity
                    sp = s[p * BLK:(p + 1) * BLK] + bias_ref[first, h]
                    m = jnp.maximum(jnp.max(sp, axis=1, keepdims=True), sink_ref[layer, h] * LOG2E)
                    probs.append(jnp.exp2(sp - m).astype(BF16))
                    mrow.append(m)
                pmat = jnp.concatenate(probs, axis=0)
                res.append(jnp.dot(pmat, v_rhs[parity], preferred_element_type=F32))
                mx.append(mrow)

            for p in range(GQA // 2):
                h_e = kv * GQA + 2 * p
                r_e = res[0][p * BLK:(p + 1) * BLK]
                r_o = res[1][p * BLK:(p + 1) * BLK]
                numer = jnp.where(lane_o, r_e, r_o)
                sums = pltpu.roll(jnp.where(lane_o, r_o, r_e), HEAD_DIM, 1)
                sink_e = jnp.exp2(sink_ref[layer, h_e] * LOG2E - mx[0][p])
                sink_o = jnp.exp2(sink_ref[layer, h_e + 1] * LOG2E - mx[1][p])
                denom = sums + jnp.where(lane_o, sink_e, sink_o)
                o_ref[pl.ds(r0, BLK), base + p * LANES: base + (p + 1) * LANES] = (numer / denom).astype(o_ref.dtype)
        return carry

    lax.fori_loop(0, n_qb, body, 0, unroll=True)


def _attention(a, sinks, layer):
    tq = ATTN_TQ
    q_blk = COL_Q // ATTN_WIDTH
    kv_blk = COL_K // (2 * KV_WIDTH)
    return pl.pallas_call(
        functools.partial(_attn_kernel, layer=layer, tq=tq),
        grid=(TOKENS // tq,),
        in_specs=[pl.BlockSpec(memory_space=pltpu.SMEM),
                  pl.BlockSpec((tq, ATTN_WIDTH), lambda i: (i, q_blk)),
                  pl.BlockSpec((tq, 2 * KV_WIDTH), lambda i: (i, kv_blk)),
                  pl.BlockSpec((BLK, 2 * KV_WIDTH), lambda i: (jnp.maximum(i * (tq // BLK) - 1, 0), kv_blk))],
        out_specs=pl.BlockSpec((tq, ATTN_WIDTH), lambda i: (i, 0)),
        out_shape=jax.ShapeDtypeStruct((TOKENS, ATTN_WIDTH), BF16),
        scratch_shapes=[pltpu.VMEM((BLK + tq, 2 * KV_WIDTH), BF16),
                        pltpu.VMEM((2, N_HEADS, BLK, 2 * BLK), F32)],
        compiler_params=_params("arbitrary"),
        name="swa_attention",
    )(sinks, a, a, a)


POOL_HALO = 16
CONV_HALO = SUBLANES_F32
C_BLK = IN_PROJ_TN
C_CHUNK = 256
MERGE_CHUNKS = 2


def _mix_kernel(u_ref, c0_ref, c1_ref, c2_ref, c3_ref, att_ref, g_ref, wgrp_ref, wpo_ref, wao_ref, wco_ref,
                scale_ref, wcm_ref, out_ref, uext_ref, zext_ref, pa_ref, pg_ref, cv_ref, *, tm):
    i = pl.program_id(0)
    seq_start = (i % (SEQ // tm)) == 0
    c_refs = (c0_ref, c1_ref, c2_ref, c3_ref)

    def conv_cols(col, width):
        blk, off = divmod(col, C_BLK)
        assert off + width <= C_BLK
        return c_refs[blk][:, off:off + width].astype(F32)

    @pl.when(seq_start)
    def _():
        uext_ref[0:POOL_HALO, :] = jnp.zeros((POOL_HALO, POOL_WIDTH), BF16)
        zext_ref[0:CONV_HALO, :] = jnp.zeros((CONV_HALO, CONV_WIDTH), F32)

    @pl.when(jnp.logical_not(seq_start))
    def _():
        uext_ref[0:POOL_HALO, :] = uext_ref[tm:tm + POOL_HALO, :]
        zext_ref[0:CONV_HALO, :] = zext_ref[tm:tm + CONV_HALO, :]

    uext_ref[POOL_HALO:, :] = u_ref[...]

    r_idx = lax.broadcasted_iota(jnp.int32, (BLK, BLK + POOL_HALO), 0)
    c_idx = lax.broadcasted_iota(jnp.int32, (BLK, BLK + POOL_HALO), 1)
    back = r_idx + POOL_HALO - c_idx
    bands = [jnp.where((back >= 0) & (back < w), 1.0, 0.0).astype(BF16) for w in POOL_WINDOWS]
    for rb in range(tm // BLK):
        pos = (i * tm + rb * BLK) % SEQ + lax.broadcasted_iota(jnp.int32, (BLK, 1), 0)
        for gi, w in enumerate(POOL_WINDOWS):
            cs = slice(gi * POOL_GROUP, (gi + 1) * POOL_GROUP)
            win = jnp.dot(bands[gi], uext_ref[rb * BLK:(rb + 1) * BLK + POOL_HALO, cs], preferred_element_type=F32)
            inv_cnt = 1.0 / jnp.minimum(pos + 1, w).astype(F32)
            tok = uext_ref[POOL_HALO + rb * BLK:POOL_HALO + (rb + 1) * BLK, cs].astype(F32)
            pa_ref[rb * BLK:(rb + 1) * BLK, cs] = (win * inv_cnt - tok).astype(BF16)
    for gi in range(len(POOL_WINDOWS)):
        cs = slice(gi * POOL_GROUP, (gi + 1) * POOL_GROUP)
        y = jnp.dot(pa_ref[:, cs], wgrp_ref[gi], preferred_element_type=F32) * scale_ref[:, cs]
        pg_ref[:, cs] = y.astype(BF16)

    for q in range(CONV_WIDTH // C_CHUNK):
        cs = slice(q * C_CHUNK, (q + 1) * C_CHUNK)
        zext_ref[CONV_HALO:, cs] = (conv_cols(CONV_WIDTH + q * C_CHUNK, C_CHUNK)
                                    * conv_cols(2 * CONV_WIDTH + q * C_CHUNK, C_CHUNK))
    for q in range(CONV_WIDTH // C_CHUNK):
        cs = slice(q * C_CHUNK, (q + 1) * C_CHUNK)
        ext = zext_ref[:, cs]
        y = (wcm_ref[0:1, cs] * pltpu.roll(ext, 2, 0)[CONV_HALO:]
             + wcm_ref[1:2, cs] * pltpu.roll(ext, 1, 0)[CONV_HALO:]
             + wcm_ref[2:3, cs] * ext[CONV_HALO:])
        cv_ref[:, cs] = (conv_cols(q * C_CHUNK, C_CHUNK) * y).astype(BF16)

    half = D_MODEL // MERGE_CHUNKS
    for nc in range(MERGE_CHUNKS):
        cs = slice(nc * half, (nc + 1) * half)
        ya = jnp.dot(pg_ref[...], wpo_ref[:, cs], preferred_element_type=F32)
        yb = jnp.dot(att_ref[...], wao_ref[:, cs], preferred_element_type=F32)
        yc = None
        for q in range(CONV_WIDTH // C_CHUNK):
            ks = slice(q * C_CHUNK, (q + 1) * C_CHUNK)
            part = jnp.dot(cv_ref[:, ks], wco_ref[ks, cs], preferred_element_type=F32)
            yc = part if yc is None else yc + part
        ga = g_ref[:, nc * half:(nc + 1) * half].astype(F32)
        gb = g_ref[:, D_MODEL + nc * half:D_MODEL + (nc + 1) * half].astype(F32)
        gc = g_ref[:, 2 * D_MODEL + nc * half:2 * D_MODEL + (nc + 1) * half].astype(F32)
        out_ref[:, cs] = (ga * ya + gb * yb + gc * yc).astype(out_ref.dtype)


def _mix(a, att, gates, wgrp, wpo, wao, wco, pool_scale3, w_conv_mix, layer):
    tm = MIX_TM
    c_blk0 = COL_CB // C_BLK
    assert COL_CB % C_BLK == 0 and (3 * CONV_WIDTH) % C_BLK == 0
    c_specs = [pl.BlockSpec((tm, C_BLK), functools.partial(lambda i, k: (i, c_blk0 + k), k=k)) for k in range(4)]
    return pl.pallas_call(
        functools.partial(_mix_kernel, tm=tm),
        grid=(TOKENS // tm,),
        in_specs=[pl.BlockSpec((tm, POOL_WIDTH), lambda i: (i, 0))] + c_specs + [
            pl.BlockSpec((tm, ATTN_WIDTH), lambda i: (i, 0)),
            pl.BlockSpec((tm, N_GATE), lambda i: (i, 0)),
            _resident((None, len(POOL_WINDOWS), POOL_GROUP, POOL_GROUP), lambda i: (layer, 0, 0, 0)),
            _resident((None, POOL_WIDTH, D_MODEL), lambda i: (layer, 0, 0)),
            _resident((None, ATTN_WIDTH, D_MODEL), lambda i: (layer, 0, 0)),
            _resident((None, CONV_WIDTH, D_MODEL), lambda i: (layer, 0, 0)),
            _resident((None, 1, POOL_WIDTH), lambda i: (layer, 0, 0)),
            _resident((None, 3, CONV_WIDTH), lambda i: (layer, 0, 0)),
        ],
        out_specs=pl.BlockSpec((tm, D_MODEL), lambda i: (i, 0)),
        out_shape=jax.ShapeDtypeStruct((TOKENS, D_MODEL), BF16),
        scratch_shapes=[pltpu.VMEM((POOL_HALO + tm, POOL_WIDTH), BF16),
                        pltpu.VMEM((CONV_HALO + tm, CONV_WIDTH), F32),
                        pltpu.VMEM((tm, POOL_WIDTH), BF16),
                        pltpu.VMEM((tm, POOL_WIDTH), BF16),
                        pltpu.VMEM((tm, CONV_WIDTH), BF16)],
        compiler_params=_params("arbitrary"),
        name="mixers_merge",
    )(a, a, a, a, a, att, gates, wgrp, wpo, wao, wco, pool_scale3, w_conv_mix)


def _proj_residual_kernel(a_ref, x_ref, w_ref, g_ref, *out_refs, final):
    x_new = x_ref[...] + jnp.dot(a_ref[...], w_ref[...], preferred_element_type=F32)
    normed = _rms_scale(x_new, g_ref[...])
    if final:
        out_refs[0][...] = normed
    else:
        out_refs[0][...] = x_new
        out_refs[1][...] = normed.astype(out_refs[1].dtype)


def _proj_residual(act, x, w, g3, w_layer, g_layer, tm, final, name):
    k = act.shape[1]
    tok_spec = pl.BlockSpec((tm, D_MODEL), lambda i: (i, 0))
    if final:
        out_specs = tok_spec
        out_shape = jax.ShapeDtypeStruct((TOKENS, D_MODEL), F32)
    else:
        out_specs = [tok_spec, tok_spec]
        out_shape = [jax.ShapeDtypeStruct((TOKENS, D_MODEL), F32), jax.ShapeDtypeStruct((TOKENS, D_MODEL), BF16)]
    g_spec = (_resident((1, D_MODEL), lambda i: (0, 0)) if g_layer is None
              else _resident((None, 1, D_MODEL), lambda i: (g_layer, 0, 0)))
    return pl.pallas_call(
        functools.partial(_proj_residual_kernel, final=final),
        grid=(TOKENS // tm,),
        in_specs=[pl.BlockSpec((tm, k), lambda i: (i, 0)),
                  tok_spec,
                  _resident((None, k, D_MODEL), lambda i: (w_layer, 0, 0)),
                  g_spec],
        out_specs=out_specs,
        out_shape=out_shape,
        compiler_params=_params("arbitrary"),
        name=name,
    )(act, x, w, g3)


def _up_kernel(h_ref, wg_ref, wv_ref, cwg_ref, cwv_ref, bg_ref, bv_ref, o_ref, wbf_ref, zext_ref, *, tm, tn):
    m = pl.program_id(1)

    @pl.when(m == 0)
    def _():
        wbf_ref[:, 0:tn] = wg_ref[...].astype(BF16)
        wbf_ref[:, tn:2 * tn] = wv_ref[...].astype(BF16)

    seq_start = (m % (SEQ // tm)) == 0

    @pl.when(seq_start)
    def _():
        zext_ref[0:CONV_HALO, :] = jnp.zeros((CONV_HALO, 2 * tn), F32)

    @pl.when(jnp.logical_not(seq_start))
    def _():
        zext_ref[0:CONV_HALO, :] = zext_ref[tm:tm + CONV_HALO, :]

    zext_ref[CONV_HALO:, :] = jnp.dot(h_ref[...], wbf_ref[...], preferred_element_type=F32)

    def conv(cs, cw_ref, b_ref, scale):
        taps = cw_ref[...] * scale
        ext = zext_ref[:, cs]
        back1 = pltpu.roll(ext, 1, 0)[CONV_HALO:]
        back2 = pltpu.roll(ext, 2, 0)[CONV_HALO:]
        return (taps[0:1] * back2 + taps[1:2] * back1 + taps[2:3] * ext[CONV_HALO:] + b_ref[...] * scale)

    half_gate = conv(slice(0, tn), cwg_ref, bg_ref, 0.5)
    u_val = conv(slice(tn, 2 * tn), cwv_ref, bv_ref, 1.0)
    o_ref[...] = (half_gate * (1.0 + jnp.tanh(half_gate)) * u_val).astype(o_ref.dtype)


def _up(h2, w_up, w_ffn_conv, b_ffn_conv3, layer):
    tm, tn = UP_TM, UP_TN
    nj = D_FF // tn
    w_spec = lambda off: pl.BlockSpec((None, D_MODEL, tn), lambda j, m: (layer, 0, j + off),
                                      pipeline_mode=pl.Buffered(1))
    return pl.pallas_call(
        functools.partial(_up_kernel, tm=tm, tn=tn),
        grid=(nj, TOKENS // tm),
        in_specs=[pl.BlockSpec((tm, D_MODEL), lambda j, m: (m, 0)),
                  w_spec(0),
                  w_spec(nj),
                  pl.BlockSpec((None, 3, tn), lambda j, m: (layer, 0, j)),
                  pl.BlockSpec((None, 3, tn), lambda j, m: (layer, 0, j + nj)),
                  pl.BlockSpec((None, 1, tn), lambda j, m: (layer, 0, j)),
                  pl.BlockSpec((None, 1, tn), lambda j, m: (layer, 0, j + nj))],
        out_specs=pl.BlockSpec((tm, tn), lambda j, m: (m, j)),
        out_shape=jax.ShapeDtypeStruct((TOKENS, D_FF), BF16),
        scratch_shapes=[pltpu.VMEM((D_MODEL, 2 * tn), BF16),
                        pltpu.VMEM((CONV_HALO + tm, 2 * tn), F32)],
        compiler_params=_params("arbitrary", "arbitrary"),
        name="up_conv_act",
    )(h2, w_up, w_up, w_ffn_conv, w_ffn_conv, b_ffn_conv3, b_ffn_conv3)


def kernel(x, g_mix, w_in, b_gate, w_pool_grp, pool_scale, w_pool_out, sinks, w_attn_out, w_conv_mix, w_conv_out,
           w_o, g_ffn, w_up, w_ffn_conv, b_ffn_conv, w_down, g_final):
    assert x.shape == (BATCH, SEQ, D_MODEL) and x.dtype == F32
    xt = x.reshape(TOKENS, D_MODEL)
    g_mix3 = g_mix.reshape(DEPTH, 1, D_MODEL)
    g_ffn3 = g_ffn.reshape(DEPTH, 1, D_MODEL)
    g_final2 = g_final.reshape(1, D_MODEL)
    b_gate3 = b_gate.reshape(DEPTH, 1, N_GATE)
    pool_scale3 = pool_scale.reshape(DEPTH, 1, POOL_WIDTH)
    b_ffn3 = b_ffn_conv.reshape(DEPTH, 1, 2 * D_FF)
    wgrp, wpo, wao, wco, wo, wdn = (w.astype(BF16) for w in (w_pool_grp, w_pool_out, w_attn_out, w_conv_out, w_o, w_down))

    h = _rmsnorm_bf16(xt, g_mix3, 0)
    for layer in range(DEPTH):
        a = _in_proj(h, w_in, None, layer, gates=False)
        gates = _in_proj(h, w_in, b_gate3, layer, gates=True)
        att = _attention(a, sinks, layer)
        merged = _mix(a, att, gates, wgrp, wpo, wao, wco, pool_scale3, w_conv_mix, layer)
        xt, h2 = _proj_residual(merged, xt, wo, g_ffn3, layer, layer, OPROJ_TM, False, "out_proj_norm")
        act = _up(h2, w_up, w_ffn_conv, b_ffn3, layer)
        if layer + 1 < DEPTH:
            xt, h = _proj_residual(act, xt, wdn, g_mix3, layer, layer + 1, DOWN_TM, False, "down_proj_norm")
        else:
            out = _proj_residual(act, xt, wdn, g_final2, layer, None, DOWN_FINAL_TM, True, "down_proj_final")
    return out.reshape(BATCH, SEQ, D_MODEL)
```
